```python
import jax, jax.numpy as jnp
from jax import lax
import numpy as np

D_MODEL = 2048
BATCH = 8
SEQ = 2048
DEPTH = 2

HEAD_DIM = 64
A_HEADS = 8
A_WIDTH = A_HEADS * HEAD_DIM
A_PATTERNS = ((128, 1), (512, 4), (2048, 16))
B_HEADS = 8
B_WIDTH = B_HEADS * HEAD_DIM
MOBA_BLOCK = 256
MOBA_TOPK = 3
MOBA_QCHUNK = 16
C_GROUPS = 8
C_GROUP_DIM = 64
C_WIDTH = C_GROUPS * C_GROUP_DIM
C_CHUNK = 128
N_BRANCHES = 3
IN_WIDTHS = (A_WIDTH, A_WIDTH, A_WIDTH, B_WIDTH, B_WIDTH, B_WIDTH, C_WIDTH, C_WIDTH, D_MODEL, D_MODEL, D_MODEL)
IN_COLS = 3 * A_WIDTH + 3 * B_WIDTH + 2 * C_WIDTH + N_BRANCHES * D_MODEL
MEM_LEN = 256
X_HEADS = 4
X_HEAD_DIM = 128
X_WIDTH = X_HEADS * X_HEAD_DIM
D_FF = 4 * D_MODEL
LN_EPS = 1e-5
DEEPNORM_ALPHA = (2 * DEPTH) ** 0.25
DEEPNORM_BETA = (8 * DEPTH) ** -0.25

kernel_name = 'hybrid_gated_dilated_moba_gmlp_block'


def _layer_norm(x, g, b):
    xf = x.astype(jnp.float32)
    mu = jnp.mean(xf, axis=-1, keepdims=True)
    var = jnp.mean(jnp.square(xf - mu), axis=-1, keepdims=True)
    y = (xf - mu) * lax.rsqrt(var + LN_EPS)
    return (y * g + b).astype(x.dtype)


def _dilated_window_stats(q, k, v, window, dilation):
    B, S, H, Dh = q.shape
    n = window // dilation
    span = n * dilation
    s_pad = -(-S // span) * span
    L = s_pad // dilation
    nb = L // n

    def blocks(t):
        t = jnp.pad(t, ((0, 0), (0, s_pad - S), (0, 0), (0, 0)))
        t = t.reshape(B, L, dilation, H, Dh).transpose(0, 2, 1, 3, 4)
        return t.reshape(B, dilation, nb, n, H, Dh)

    def with_prev(t):
        prev = jnp.concatenate([jnp.zeros_like(t[:, :, :1]), t[:, :, :-1]], axis=2)
        return jnp.concatenate([prev, t], axis=3)

    qb = blocks(q)
    k2 = with_prev(blocks(k))
    v2 = with_prev(blocks(v))
    s = jnp.einsum('brnqhd,brnkhd->brnhqk', qb, k2, preferred_element_type=jnp.float32)
    qi = jnp.arange(n)[:, None]
    kj = jnp.arange(2 * n)[None, :]
    dist = qi + n - kj
    band = (dist >= 0) & (dist <= n)
    has_prev = (jnp.arange(nb) > 0)[:, None, None] | (kj >= n)[None]
    mask = band[None] & has_prev
    s = jnp.where(mask[None, None, :, None], s, -jnp.inf)
    m = jnp.max(s, axis=-1)
    p = jnp.exp(s - m[..., None])
    l = jnp.sum(p, axis=-1)
    acc = jnp.einsum('brnhqk,brnkhd->brnqhd', p, v2.astype(jnp.float32))

    def unblock_stat(t):
        return t.transpose(0, 2, 4, 1, 3).reshape(B, s_pad, H)[:, :S]

    acc = acc.transpose(0, 2, 3, 1, 4, 5).reshape(B, s_pad, H, Dh)[:, :S]
    return unblock_stat(m), unblock_stat(l), acc


def _dilated_attention(q, k, v):
    q = q * (q.shape[-1] ** -0.5)
    stats = [_dilated_window_stats(q, k, v, w, d) for (w, d) in A_PATTERNS]
    m_all = jnp.stack([st[0] for st in stats])
    l_all = jnp.stack([st[1] for st in stats])
    acc_all = jnp.stack([st[2] for st in stats])
    wgt = jnp.exp(m_all - jnp.max(m_all, axis=0, keepdims=True))
    num = jnp.sum(wgt[..., None] * acc_all, axis=0)
    den = jnp.sum(wgt * l_all, axis=0)
    return (num / den[..., None]).astype(v.dtype)


def _moba_attention(q, k, v):
    B, S, H, Dh = q.shape
    n_blk = -(-S // MOBA_BLOCK)
    s_pad = n_blk * MOBA_BLOCK
    padw = ((0, 0), (0, s_pad - S), (0, 0), (0, 0))
    qh = jnp.pad(q, padw).transpose(0, 2, 1, 3) * (Dh ** -0.5)
    kb = jnp.pad(k, padw).transpose(0, 2, 1, 3).reshape(B, H, n_blk, MOBA_BLOCK, Dh)
    vb = jnp.pad(v, padw).transpose(0, 2, 1, 3).reshape(B, H, n_blk, MOBA_BLOCK, Dh)
    k_mean = jnp.mean(kb.astype(jnp.float32), axis=3)
    topk = min(MOBA_TOPK, n_blk)
    b_idx = jnp.arange(B)[:, None, None, None]
    h_idx = jnp.arange(H)[None, :, None, None]
    local = jnp.arange(MOBA_BLOCK)

    def one_chunk(c):
        p0 = c * MOBA_QCHUNK
        blk = p0 // MOBA_BLOCK
        qc = lax.dynamic_slice_in_dim(qh, p0, MOBA_QCHUNK, axis=2)
        gate = jnp.einsum('bhqd,bhnd->bhqn', qc.astype(jnp.float32), k_mean)
        gate = jnp.where(jnp.arange(n_blk) < blk, gate, -jnp.inf)
        _, sel = lax.top_k(gate, topk)
        sel_ok = sel < blk
        k_sel = kb[b_idx, h_idx, sel]
        v_sel = vb[b_idx, h_idx, sel]
        s_sel = jnp.einsum('bhqd,bhqjkd->bhqjk', qc, k_sel, preferred_element_type=jnp.float32)
        s_sel = jnp.where(sel_ok[..., None], s_sel, -jnp.inf).reshape(B, H, MOBA_QCHUNK, topk * MOBA_BLOCK)
        k_own = lax.dynamic_index_in_dim(kb, blk, axis=2, keepdims=False)
        v_own = lax.dynamic_index_in_dim(vb, blk, axis=2, keepdims=False)
        s_own = jnp.einsum('bhqd,bhkd->bhqk', qc, k_own, preferred_element_type=jnp.float32)
        q_pos = p0 + jnp.arange(MOBA_QCHUNK)
        k_pos = blk * MOBA_BLOCK + local
        s_own = jnp.where(k_pos[None, :] <= q_pos[:, None], s_own, -jnp.inf)
        p = jax.nn.softmax(jnp.concatenate([s_sel, s_own], axis=-1), axis=-1)
        p_sel = p[..., :topk * MOBA_BLOCK].reshape(B, H, MOBA_QCHUNK, topk, MOBA_BLOCK)
        p_own = p[..., topk * MOBA_BLOCK:]
        out = (jnp.einsum('bhqjk,bhqjkd->bhqd', p_sel, v_sel.astype(jnp.float32))
               + jnp.einsum('bhqk,bhkd->bhqd', p_own, v_own.astype(jnp.float32)))
        return out.astype(v.dtype)

    outs = lax.map(one_chunk, jnp.arange(s_pad // MOBA_QCHUNK))
    return outs.transpose(1, 0, 3, 2, 4).reshape(B, s_pad, H, Dh)[:, :S]


def _chunked_spatial_gating(u, v, ln_g, ln_b, w_s, b_s):
    B, S, _ = u.shape
    v = _layer_norm(v, ln_g, ln_b).reshape(B, S // C_CHUNK, C_CHUNK, C_GROUPS, C_GROUP_DIM)
    causal = jnp.tril(jnp.ones((C_CHUNK, C_CHUNK), dtype=bool))
    w = jnp.where(causal, w_s, 0.0)
    sv = jnp.einsum('gts,bnsgc->bntgc', w, v) + b_s.T[None, None, :, :, None]
    return u * sv.reshape(B, S, C_WIDTH)


def _hybrid_mixer(h, w_in, w_br_a, w_br_b, w_br_c, w_out, c_ln_g, c_ln_b, c_ws, c_bs):
    B, S, _ = h.shape
    splits = np.cumsum(IN_WIDTHS)[:-1].tolist()
    qa, ka, va, qb, kb, vb, uc, vc, ga, gb, gc = jnp.split(h @ w_in, splits, axis=-1)
    a_heads = lambda t: t.reshape(B, S, A_HEADS, HEAD_DIM)
    b_heads = lambda t: t.reshape(B, S, B_HEADS, HEAD_DIM)
    ya = _dilated_attention(a_heads(qa), a_heads(ka), a_heads(va)).reshape(B, S, A_WIDTH) @ w_br_a
    yb = _moba_attention(b_heads(qb), b_heads(kb), b_heads(vb)).reshape(B, S, B_WIDTH) @ w_br_b
    yc = _chunked_spatial_gating(jax.nn.gelu(uc), jax.nn.gelu(vc), c_ln_g, c_ln_b, c_ws, c_bs) @ w_br_c
    merged = jax.nn.sigmoid(ga) * ya + jax.nn.sigmoid(gb) * yb + jax.nn.sigmoid(gc) * yc
    return merged @ w_out


def _memory_cross_attention(h, mem, w_q, w_k, w_v, w_o):
    B, S, _ = h.shape
    M = mem.shape[1]
    q = (h @ w_q).reshape(B, S, X_HEADS, X_HEAD_DIM) * (X_HEAD_DIM ** -0.5)
    k = (mem @ w_k).reshape(B, M, X_HEADS, X_HEAD_DIM)
    v = (mem @ w_v).reshape(B, M, X_HEADS, X_HEAD_DIM)
    s = jnp.einsum('bshd,bmhd->bhsm', q, k, preferred_element_type=jnp.float32)
    p = jax.nn.softmax(s, axis=-1)
    o = jnp.einsum('bhsm,bmhd->bshd', p, v.astype(jnp.float32)).astype(h.dtype)
    return o.reshape(B, S, X_WIDTH) @ w_o


def _squared_relu_mlp(h, w1, w2):
    return jnp.square(jax.nn.relu(h @ w1)) @ w2


def setup_inputs(seed: int = 0) -> dict:
    key = jax.random.key(seed)
    ks = jax.random.split(key, 24)
    L = DEPTH

    def normal(k, shape, scale):
        return jax.random.normal(k, shape, dtype=jnp.float32) * scale

    def gain(k, shape):
        return 1.0 + normal(k, shape, 0.02)

    return {
        'x': normal(ks[0], (BATCH, SEQ, D_MODEL), 1.0),
        'mem': normal(ks[1], (BATCH, MEM_LEN, D_MODEL), 1.0),
        'w_in': normal(ks[2], (L, D_MODEL, IN_COLS), D_MODEL ** -0.5),
        'w_br_a': normal(ks[3], (L, A_WIDTH, D_MODEL), A_WIDTH ** -0.5),
        'w_br_b': normal(ks[4], (L, B_WIDTH, D_MODEL), B_WIDTH ** -0.5),
        'w_br_c': normal(ks[5], (L, C_WIDTH, D_MODEL), C_WIDTH ** -0.5),
        'w_out': normal(ks[6], (L, D_MODEL, D_MODEL), DEEPNORM_BETA * D_MODEL ** -0.5),
        'c_ln_g': gain(ks[7], (L, C_WIDTH)),
        'c_ln_b': normal(ks[8], (L, C_WIDTH), 0.02),
        'c_ws': normal(ks[9], (L, C_GROUPS, C_CHUNK, C_CHUNK), 0.5 * C_CHUNK ** -0.5),
        'c_bs': gain(ks[10], (L, C_GROUPS, C_CHUNK)),
        'ln1_g': gain(ks[11], (L, D_MODEL)),
        'ln1_b': normal(ks[12], (L, D_MODEL), 0.02),
        'w_xq': normal(ks[13], (L, D_MODEL, X_WIDTH), D_MODEL ** -0.5),
        'w_xk': normal(ks[14], (L, D_MODEL, X_WIDTH), D_MODEL ** -0.5),
        'w_xv': normal(ks[15], (L, D_MODEL, X_WIDTH), D_MODEL ** -0.5),
        'w_xo': normal(ks[16], (L, X_WIDTH, D_MODEL), DEEPNORM_BETA * X_WIDTH ** -0.5),
        'ln2_g': gain(ks[17], (L, D_MODEL)),
        'ln2_b': normal(ks[18], (L, D_MODEL), 0.02),
        'w_ff1': normal(ks[19], (L, D_MODEL, D_FF), D_MODEL ** -0.5),
        'w_ff2': normal(ks[20], (L, D_FF, D_MODEL), DEEPNORM_BETA * D_FF ** -0.5),
        'ln3_g': gain(ks[21], (L, D_MODEL)),
        'ln3_b': normal(ks[22], (L, D_MODEL), 0.02),
    }


def reference(x, mem, w_in, w_br_a, w_br_b, w_br_c, w_out, c_ln_g, c_ln_b, c_ws, c_bs,
              ln1_g, ln1_b, w_xq, w_xk, w_xv, w_xo, ln2_g, ln2_b, w_ff1, w_ff2, ln3_g, ln3_b):
    for i in range(DEPTH):
        mix = _hybrid_mixer(x, w_in[i], w_br_a[i], w_br_b[i], w_br_c[i], w_out[i],
                            c_ln_g[i], c_ln_b[i], c_ws[i], c_bs[i])
        x = _layer_norm(DEEPNORM_ALPHA * x + mix, ln1_g[i], ln1_b[i])
        cross = _memory_cross_attention(x, mem, w_xq[i], w_xk[i], w_xv[i], w_xo[i])
        x = _layer_norm(DEEPNORM_ALPHA * x + cross, ln2_g[i], ln2_b[i])
        ff = _squared_relu_mlp(x, w_ff1[i], w_ff2[i])
        x = _layer_norm(DEEPNORM_ALPHA * x + ff, ln3_g[i], ln3_b[i])
    return x
```

```python
import functools

import jax
import jax.numpy as jnp
from jax import lax
from jax.experimental import pallas as pl
from jax.experimental.pallas import tpu as pltpu

F32 = jnp.float32
BF16 = jnp.bfloat16

HEAD_DIM = 64
A_PATTERNS = ((128, 1), (512, 4), (2048, 16))
ATT_WIDTH = 512
MOBA_BLOCK = 256
MOBA_TOPK = 3
C_GROUPS = 8
C_GROUP_DIM = 64
C_WIDTH = C_GROUPS * C_GROUP_DIM
C_CHUNK = 128
X_HEADS = 4
X_HEAD_DIM = 128
X_WIDTH = X_HEADS * X_HEAD_DIM
LN_EPS = 1e-5

LANES = 128
ATT_BLOCK = 256
ONES_ROWS = 16
VMEM_LIMIT = 56 * 1024 * 1024


def _params(*sem):
    return pltpu.CompilerParams(dimension_semantics=sem, vmem_limit_bytes=VMEM_LIMIT)


def _resident_spec(shape, index_map):
    return pl.BlockSpec(shape, index_map, pipeline_mode=pl.Buffered(1))


def _layer_norm(z, g, b):
    mu = jnp.mean(z, axis=-1, keepdims=True)
    zc = z - mu
    var = jnp.mean(zc * zc, axis=-1, keepdims=True)
    return zc * lax.rsqrt(var + LN_EPS) * g + b


def _gelu_tanh(x):
    c = (2.0 / jnp.pi) ** 0.5
    return 0.5 * x * (1.0 + jnp.tanh(c * (x + 0.044715 * (x * x * x))))


def _dot(a, b):
    return jnp.dot(a, b, preferred_element_type=F32)


def _dot_nt(a, b):
    return lax.dot_general(a, b, (((1,), (1,)), ((), ())), preferred_element_type=F32)


def _matmul_kernel(a_ref, w_ref, o_ref):
    o_ref[...] = _dot(a_ref[...], w_ref[...]).astype(o_ref.dtype)


def _matmul(a, w, out_dtype, tm, tn, name):
    m, k = a.shape
    n = w.shape[1]
    tm, tn = min(tm, m), min(tn, n)
    assert m % tm == 0 and n % tn == 0
    return pl.pallas_call(
        _matmul_kernel,
        grid=(m // tm, n // tn),
        in_specs=[pl.BlockSpec((tm, k), lambda i, j: (i, 0)),
                  pl.BlockSpec((k, tn), lambda i, j: (0, j))],
        out_specs=pl.BlockSpec((tm, tn), lambda i, j: (i, j)),
        out_shape=jax.ShapeDtypeStruct((m, n), out_dtype),
        compiler_params=_params("parallel", "parallel"),
        name=name,
    )(a, w)


def _attn_kernel(*refs, mode, n_blocks):
    if mode == "dilated":
        q_ref, k_ref, vt_ref, bias_ref, mult_ref, o_ref = refs
    else:
        q_ref, k_ref, vt_ref, o_ref, kmean_ref, sel_ref = refs
    tb = ATT_BLOCK
    i = pl.program_id(2)
    q2 = q_ref[...] * jnp.asarray(HEAD_DIM ** -0.5, BF16)
    lane = lax.broadcasted_iota(jnp.int32, q2.shape, 1)
    zero = jnp.zeros_like(q2)
    qm = (jnp.where(lane < HEAD_DIM, q2, zero), jnp.where(lane >= HEAD_DIM, q2, zero))
    ones_rows = jnp.ones((ONES_ROWS, tb), BF16)

    if mode == "moba":
        @pl.when(i == 0)
        def _():
            kmean_ref[...] = jnp.zeros_like(kmean_ref)
            for n in range(n_blocks):
                kb = k_ref[n * tb:(n + 1) * tb, :].astype(F32)
                kmean_ref[n:n + 1, :] = jnp.sum(kb, axis=0, keepdims=True) / tb

        kmean = kmean_ref[...]
        klane = lax.broadcasted_iota(jnp.int32, kmean.shape, 1)
        row = lax.broadcasted_iota(jnp.int32, (kmean.shape[0], tb), 0)
        for h in range(2):
            km = jnp.where((klane < HEAD_DIM) == (h == 0), kmean, 0.0)
            km_hi = km.astype(BF16)
            km_lo = (km - km_hi.astype(F32)).astype(BF16)
            gate = _dot_nt(km_hi, q2) + _dot_nt(km_lo, q2)
            gate = jnp.where(row < i, gate, -jnp.inf)
            for n in range(n_blocks):
                gn = gate[n:n + 1, :]
                beats = jnp.where(gate > gn, 1.0, 0.0) + jnp.where((gate == gn) & (row < n), 1.0, 0.0)
                cnt = jnp.sum(beats, axis=0, keepdims=True)
                past = (n < i).astype(F32)
                sel_ref[h, n:n + 1, :] = jnp.where(cnt < MOBA_TOPK, past, 0.0)

    def block(j, carry, mask_fn):
        start = pl.multiple_of(j * tb, tb)
        kj = k_ref[pl.ds(start, tb), :]
        vtj = vt_ref[j]
        out = []
        for h in range(2):
            m, acc = carry[2 * h], carry[2 * h + 1]
            s = _dot_nt(kj, qm[h])
            s, mult = mask_fn(s, h)
            m_new = jnp.maximum(m, jnp.max(s, axis=0, keepdims=True))
            alpha = jnp.exp(m - m_new)
            p = jnp.exp(s - m_new)
            if mult is not None:
                p = p * mult
            vext = jnp.concatenate([vtj[h * HEAD_DIM:(h + 1) * HEAD_DIM, :], ones_rows], axis=0)
            acc = alpha * acc + _dot(vext, p.astype(BF16))
            out += [m_new, acc]
        return tuple(out)

    if mode == "dilated":
        def mask_at(j):
            d = i - j
            return lambda s, h: (s + bias_ref[d], mult_ref[d])
        diag_mask = mask_at(i)
        past_mask = mask_at
    else:
        key_pos = lax.broadcasted_iota(jnp.int32, (tb, tb), 0)
        qry_pos = lax.broadcasted_iota(jnp.int32, (tb, tb), 1)

        def diag_mask(s, h):
            return jnp.where(key_pos <= qry_pos, s, -jnp.inf), None

        def past_mask(j):
            return lambda s, h: (jnp.where(sel_ref[h, pl.ds(j, 1), :] > 0.0, s, -jnp.inf), None)

    init = []
    for _ in range(2):
        init += [jnp.full((1, tb), -jnp.inf, F32), jnp.zeros((HEAD_DIM + ONES_ROWS, tb), F32)]
    carry = block(i, tuple(init), diag_mask)
    carry = lax.fori_loop(0, i, lambda j, c: block(j, c, past_mask(j)), carry)
    for h in range(2):
        acc = carry[2 * h + 1]
        o = acc[:HEAD_DIM, :] / acc[HEAD_DIM:HEAD_DIM + 1, :]
        o_ref[h * HEAD_DIM:(h + 1) * HEAD_DIM, :] = o.astype(o_ref.dtype)


def _dilated_tables(n_blocks):
    tb = ATT_BLOCK
    shape = (n_blocks, tb, tb)
    d = lax.broadcasted_iota(jnp.int32, shape, 0)
    c = lax.broadcasted_iota(jnp.int32, shape, 1)
    r = lax.broadcasted_iota(jnp.int32, shape, 2)
    delta = d * tb + r - c
    mult = jnp.zeros(shape, F32)
    for window, dilation in A_PATTERNS:
        hit = (delta >= 0) & (delta <= window) & (delta % dilation == 0)
        mult = mult + hit.astype(F32)
    bias = jnp.where(mult > 0, 0.0, -jnp.inf).astype(F32)
    return bias, mult


def _attention(proj, vt, q_col, k_col, mode, name):
    b, s, _ = proj.shape
    tb = ATT_BLOCK
    n_blocks = s // tb
    n_pairs = ATT_WIDTH // LANES
    in_specs = [
        pl.BlockSpec((None, tb, LANES), lambda bi, hp, i: (bi, i, q_col + hp)),
        pl.BlockSpec((None, s, LANES), lambda bi, hp, i: (bi, 0, k_col + hp)),
        pl.BlockSpec((None, None, n_blocks, LANES, tb), lambda bi, hp, i: (bi, hp, 0, 0, 0)),
    ]
    args = [proj, proj, vt]
    scratch = []
    if mode == "dilated":
        bias, mult = _dilated_tables(n_blocks)
        table_spec = pl.BlockSpec((n_blocks, tb, tb), lambda bi, hp, i: (0, 0, 0))
        in_specs += [table_spec, table_spec]
        args += [bias, mult]
    else:
        scratch = [pltpu.VMEM((ONES_ROWS, LANES), F32), pltpu.VMEM((2, n_blocks, tb), F32)]
    return pl.pallas_call(
        functools.partial(_attn_kernel, mode=mode, n_blocks=n_blocks),
        grid=(b, n_pairs, n_blocks),
        in_specs=in_specs,
        out_specs=pl.BlockSpec((None, None, LANES, tb), lambda bi, hp, i: (bi, hp, 0, i)),
        out_shape=jax.ShapeDtypeStruct((b, n_pairs, LANES, s), BF16),
        scratch_shapes=scratch,
        compiler_params=_params("parallel", "parallel", "arbitrary"),
        name=name,
    )(*args)


def _v_transposed(proj, v_col0):
    b, s, _ = proj.shape
    tb = ATT_BLOCK
    v = proj[:, :, v_col0:v_col0 + ATT_WIDTH].reshape(b, s // tb, tb, ATT_WIDTH // LANES, LANES)
    return v.transpose(0, 3, 1, 4, 2)


def _attn_untranspose(ot):
    b, n_pairs, _, s = ot.shape
    return ot.transpose(0, 3, 1, 2).reshape(b * s, n_pairs * LANES)


def _gmlp_kernel(u_ref, v_ref, g_ref, b_ref, w_ref, bias_ref, o_ref, *, chunks):
    u = _gelu_tanh(u_ref[...].astype(F32))
    v = _layer_norm(_gelu_tanh(v_ref[...].astype(F32)), g_ref[...], b_ref[...])
    lane = lax.broadcasted_iota(jnp.int32, (C_CHUNK, LANES), 1)
    t_idx = lax.broadcasted_iota(jnp.int32, (C_CHUNK, C_CHUNK), 0)
    s_idx = lax.broadcasted_iota(jnp.int32, (C_CHUNK, C_CHUNK), 1)
    w = [jnp.where(s_idx <= t_idx, w_ref[g], 0.0).astype(BF16) for g in range(C_GROUPS)]
    per_tile = LANES // C_GROUP_DIM
    for ch in range(chunks):
        rows = slice(ch * C_CHUNK, (ch + 1) * C_CHUNK)
        for p in range(C_WIDTH // LANES):
            cols = slice(p * LANES, (p + 1) * LANES)
            v2 = v[rows, cols]
            sv = bias_ref[:, cols]
            for q in range(per_tile):
                in_group = (lane >= q * C_GROUP_DIM) & (lane < (q + 1) * C_GROUP_DIM)
                vq = jnp.where(in_group, v2, 0.0).astype(BF16)
                sv = sv + jnp.where(in_group, _dot(w[p * per_tile + q], vq), 0.0)
            o_ref[rows, cols] = (u[rows, cols] * sv).astype(o_ref.dtype)


def _gmlp(proj2d, u_col, v_col, ln_g, ln_b, w_s, b_s, tc):
    t = proj2d.shape[0]
    bias = jnp.repeat(b_s.T, C_GROUP_DIM, axis=1)
    const2 = lambda i: (0, 0)
    return pl.pallas_call(
        functools.partial(_gmlp_kernel, chunks=tc // C_CHUNK),
        grid=(t // tc,),
        in_specs=[pl.BlockSpec((tc, C_WIDTH), lambda i: (i, u_col)),
                  pl.BlockSpec((tc, C_WIDTH), lambda i: (i, v_col)),
                  pl.BlockSpec((1, C_WIDTH), const2),
                  pl.BlockSpec((1, C_WIDTH), const2),
                  pl.BlockSpec((C_GROUPS, C_CHUNK, C_CHUNK), lambda i: (0, 0, 0)),
                  pl.BlockSpec((C_CHUNK, C_WIDTH), const2)],
        out_specs=pl.BlockSpec((tc, C_WIDTH), lambda i: (i, 0)),
        out_shape=jax.ShapeDtypeStruct((t, C_WIDTH), BF16),
        compiler_params=_params("parallel"),
        name="gmlp",
    )(proj2d, proj2d, ln_g.reshape(1, -1), ln_b.reshape(1, -1), w_s, bias)


def _emit_norm(z, g_ref, b_ref, out_refs):
    y = _layer_norm(z, g_ref[...], b_ref[...])
    for o_ref in out_refs:
        o_ref[...] = y.astype(o_ref.dtype)


def _merge_kernel(oa_ref, ob_ref, yc_ref, ga_ref, gb_ref, gc_ref, x_ref,
                  wa_ref, wb_ref, wc_ref, wo_ref, g_ref, b_ref, *out_refs, alpha):
    def branch(y_ref, w_ref, gate_ref):
        return jax.nn.sigmoid(gate_ref[...].astype(F32)) * _dot(y_ref[...], w_ref[...])

    merged = branch(oa_ref, wa_ref, ga_ref) + branch(ob_ref, wb_ref, gb_ref) + branch(yc_ref, wc_ref, gc_ref)
    mix = _dot(merged.astype(BF16), wo_ref[...])
    _emit_norm(alpha * x_ref[...] + mix, g_ref, b_ref, out_refs)


def _merge(oa, ob, yc, proj2d, gate_col, x, wa, wb, wc, wo, ln_g, ln_b, alpha, tm):
    t, d = x.shape
    row = lambda i: (i, 0)
    const = lambda i: (0, 0)
    branch_spec = pl.BlockSpec((tm, ATT_WIDTH), row)
    gate_specs = [pl.BlockSpec((tm, d), lambda i, c=gate_col + n: (i, c)) for n in range(3)]
    w_br_spec = _resident_spec((ATT_WIDTH, d), const)
    vec_spec = pl.BlockSpec((1, d), const)
    return pl.pallas_call(
        functools.partial(_merge_kernel, alpha=alpha),
        grid=(t // tm,),
        in_specs=[branch_spec, branch_spec, branch_spec, *gate_specs, pl.BlockSpec((tm, d), row),
                  w_br_spec, w_br_spec, w_br_spec, _resident_spec((d, d), const), vec_spec, vec_spec],
        out_specs=[pl.BlockSpec((tm, d), row), pl.BlockSpec((tm, d), row)],
        out_shape=[jax.ShapeDtypeStruct((t, d), F32), jax.ShapeDtypeStruct((t, d), BF16)],
        compiler_params=_params("parallel"),
        name="merge",
    )(oa, ob, yc, proj2d, proj2d, proj2d, x, wa, wb, wc, wo, ln_g.reshape(1, -1), ln_b.reshape(1, -1))


def _cross_kernel(x_ref, xb_ref, k_ref, v_ref, wq_ref, wo_ref, g_ref, b_ref, *out_refs, alpha):
    q = (_dot(xb_ref[...], wq_ref[...]) * (X_HEAD_DIM ** -0.5)).astype(BF16)
    heads = []
    for h in range(X_HEADS):
        cols = slice(h * X_HEAD_DIM, (h + 1) * X_HEAD_DIM)
        s = _dot_nt(q[:, cols], k_ref[:, cols])
        p = jnp.exp(s - jnp.max(s, axis=-1, keepdims=True))
        o = _dot(p.astype(BF16), v_ref[:, cols]) / jnp.sum(p, axis=-1, keepdims=True)
        heads.append(o.astype(BF16))
    cross = _dot(jnp.concatenate(heads, axis=-1), wo_ref[...])
    _emit_norm(alpha * x_ref[...] + cross, g_ref, b_ref, out_refs)


def _cross(x, xb, kv, wq, wo, ln_g, ln_b, alpha, batch, tm):
    t, d = x.shape
    s = t // batch
    m = kv.shape[1]
    x3, xb3 = x.reshape(batch, s, d), xb.reshape(batch, s, d)
    row = lambda bi, i: (bi, i, 0)
    const = lambda bi, i: (0, 0)
    outs = pl.pallas_call(
        functools.partial(_cross_kernel, alpha=alpha),
        grid=(batch, s // tm),
        in_specs=[pl.BlockSpec((None, tm, d), row), pl.BlockSpec((None, tm, d), row),
                  pl.BlockSpec((None, m, X_WIDTH), lambda bi, i: (bi, 0, 0)),
                  pl.BlockSpec((None, m, X_WIDTH), lambda bi, i: (bi, 0, 1)),
                  pl.BlockSpec((d, X_WIDTH), const), pl.BlockSpec((X_WIDTH, d), const),
                  pl.BlockSpec((1, d), const), pl.BlockSpec((1, d), const)],
        out_specs=[pl.BlockSpec((None, tm, d), row), pl.BlockSpec((None, tm, d), row)],
        out_shape=[jax.ShapeDtypeStruct((batch, s, d), F32), jax.ShapeDtypeStruct((batch, s, d), BF16)],
        compiler_params=_params("parallel", "parallel"),
        name="cross",
    )(x3, xb3, kv, kv, wq, wo, ln_g.reshape(1, -1), ln_b.reshape(1, -1))
    return outs[0].reshape(t, d), outs[1].reshape(t, d)


def _ffn_kernel(x_ref, xb_ref, w1_ref, w2_ref, g_ref, b_ref, *refs, alpha):
    *out_refs, acc_ref = refs
    j = pl.program_id(1)
    h = jnp.maximum(_dot(xb_ref[...], w1_ref[...]), 0.0)
    part = _dot((h * h).astype(BF16), w2_ref[...])

    @pl.when(j == 0)
    def _():
        acc_ref[...] = part

    @pl.when(j > 0)
    def _():
        acc_ref[...] += part

    @pl.when(j == pl.num_programs(1) - 1)
    def _():
        _emit_norm(alpha * x_ref[...] + acc_ref[...], g_ref, b_ref, out_refs)


def _ffn(x, xb, w1, w2, ln_g, ln_b, alpha, tm, tf, with_bf16):
    t, d = x.shape
    d_ff = w1.shape[1]
    row = lambda i, j: (i, 0)
    const = lambda i, j: (0, 0)
    out_dtypes = [F32, BF16] if with_bf16 else [F32]
    return pl.pallas_call(
        functools.partial(_ffn_kernel, alpha=alpha),
        grid=(t // tm, d_ff // tf),
        in_specs=[pl.BlockSpec((tm, d), row), pl.BlockSpec((tm, d), row),
                  pl.BlockSpec((d, tf), lambda i, j: (0, j)), pl.BlockSpec((tf, d), lambda i, j: (j, 0)),
                  pl.BlockSpec((1, d), const), pl.BlockSpec((1, d), const)],
        out_specs=[pl.BlockSpec((tm, d), row) for _ in out_dtypes],
        out_shape=[jax.ShapeDtypeStruct((t, d), dt) for dt in out_dtypes],
        scratch_shapes=[pltpu.VMEM((tm, d), F32)],
        compiler_params=_params("parallel", "arbitrary"),
        name="ffn",
    )(x, xb, w1, w2, ln_g.reshape(1, -1), ln_b.reshape(1, -1))


def kernel(x, mem, w_in, w_br_a, w_br_b, w_br_c, w_out, c_ln_g, c_ln_b, c_ws, c_bs,
           ln1_g, ln1_b, w_xq, w_xk, w_xv, w_xo, ln2_g, ln2_b, w_ff1, w_ff2, ln3_g, ln3_b):
    batch, seq, d = x.shape
    depth = w_in.shape[0]
    t = batch * seq
    alpha = (2 * depth) ** 0.25
    assert all(seq % window == 0 for window, _ in A_PATTERNS) and seq % ATT_BLOCK == 0
    assert w_in.shape[2] == 3 * ATT_WIDTH * 2 + 2 * C_WIDTH + 3 * d

    qa, ka, qb, kb = 0, 4, 12, 16
    va_col, vb_col = 2 * ATT_WIDTH, 5 * ATT_WIDTH
    u_blk, v_blk = 6, 7
    gate_blk = (6 * ATT_WIDTH + 2 * C_WIDTH) // d

    xf = x.reshape(t, d)
    xb = xf.astype(BF16)
    mem_b = mem.reshape(-1, d).astype(BF16)
    for l in range(depth):
        proj2d = _matmul(xb, w_in[l].astype(BF16), BF16, 1024, 1024, "in_proj")
        proj = proj2d.reshape(batch, seq, -1)
        oa = _attention(proj, _v_transposed(proj, va_col), qa, ka, "dilated", "dilated_attn")
        ob = _attention(proj, _v_transposed(proj, vb_col), qb, kb, "moba", "moba_attn")
        yc = _gmlp(proj2d, u_blk, v_blk, c_ln_g[l], c_ln_b[l], c_ws[l], c_bs[l], 512)
        xf, xb = _merge(_attn_untranspose(oa), _attn_untranspose(ob), yc, proj2d, gate_blk, xf,
                        w_br_a[l].astype(BF16), w_br_b[l].astype(BF16), w_br_c[l].astype(BF16),
                        w_out[l].astype(BF16), ln1_g[l], ln1_b[l], alpha, 256)
        w_kv = jnp.concatenate([w_xk[l], w_xv[l]], axis=1).astype(BF16)
        kv = _matmul(mem_b, w_kv, BF16, 1024, 1024, "mem_proj").reshape(batch, -1, 2 * X_WIDTH)
        xf, xb = _cross(xf, xb, kv, w_xq[l].astype(BF16), w_xo[l].astype(BF16),
                        ln2_g[l], ln2_b[l], alpha, batch, 512)
        last = l == depth - 1
        outs = _ffn(xf, xb, w_ff1[l].astype(BF16), w_ff2[l].astype(BF16),
                    ln3_g[l], ln3_b[l], alpha, 512, 512, not last)
        xf = outs[0]
        xb = None if last else outs[1]
    return xf.reshape(batch, seq, d)
```

```python
import functools

import jax
import jax.numpy as jnp
from jax import lax
from jax.experimental import pallas as pl
from jax.experimental.pallas import tpu as pltpu

F32 = jnp.float32
BF16 = jnp.bfloat16

HEAD_DIM = 64
A_PATTERNS = ((128, 1), (512, 4), (2048, 16))
ATT_WIDTH = 512
MOBA_BLOCK = 256
MOBA_TOPK = 3
C_GROUPS = 8
C_GROUP_DIM = 64
C_WIDTH = C_GROUPS * C_GROUP_DIM
C_CHUNK = 128
X_HEADS = 4
X_HEAD_DIM = 128
X_WIDTH = X_HEADS * X_HEAD_DIM
LN_EPS = 1e-5

LANES = 128
ATT_BLOCK = 256
ONES_ROWS = 16
VMEM_LIMIT = 56 * 1024 * 1024


def _params(*sem):
    return pltpu.CompilerParams(dimension_semantics=sem, vmem_limit_bytes=VMEM_LIMIT)


def _resident_spec(shape, index_map):
    return pl.BlockSpec(shape, index_map, pipeline_mode=pl.Buffered(1))


def _layer_norm(z, g, b):
    mu = jnp.mean(z, axis=-1, keepdims=True)
    zc = z - mu
    var = jnp.mean(zc * zc, axis=-1, keepdims=True)
    return zc * lax.rsqrt(var + LN_EPS) * g + b


def _gelu_tanh(x):
    c = (2.0 / jnp.pi) ** 0.5
    return 0.5 * x * (1.0 + jnp.tanh(c * (x + 0.044715 * (x * x * x))))


def _dot(a, b):
    return jnp.dot(a, b, preferred_element_type=F32)


def _dot_nt(a, b):
    return lax.dot_general(a, b, (((1,), (1,)), ((), ())), preferred_element_type=F32)


def _matmul_kernel(a_ref, w_ref, o_ref):
    o_ref[...] = _dot(a_ref[...], w_ref[...]).astype(o_ref.dtype)


def _matmul(a, w, out_dtype, tm, tn, name):
    m, k = a.shape
    n = w.shape[1]
    tm, tn = min(tm, m), min(tn, n)
    assert m % tm == 0 and n % tn == 0
    return pl.pallas_call(
        _matmul_kernel,
        grid=(m // tm, n // tn),
        in_specs=[pl.BlockSpec((tm, k), lambda i, j: (i, 0)),
                  pl.BlockSpec((k, tn), lambda i, j: (0, j))],
        out_specs=pl.BlockSpec((tm, tn), lambda i, j: (i, j)),
        out_shape=jax.ShapeDtypeStruct((m, n), out_dtype),
        compiler_params=_params("parallel", "parallel"),
        name=name,
    )(a, w)


def _moba_selection(kmean, q2, head, n_past):
    klane = lax.broadcasted_iota(jnp.int32, kmean.shape, 1)
    km = jnp.where((klane < HEAD_DIM) == (head == 0), kmean, 0.0)
    km_hi = km.astype(BF16)
    km_lo = (km - km_hi.astype(F32)).astype(BF16)
    gate = _dot_nt(km_hi, q2) + _dot_nt(km_lo, q2)
    row = lax.broadcasted_iota(jnp.int32, gate.shape, 0)
    gate = jnp.where(row < n_past, gate, -jnp.inf)
    sel = []
    for n in range(n_past):
        gn = gate[n:n + 1, :]
        ahead = jnp.where(gate > gn, 1.0, 0.0) + jnp.where((gate == gn) & (row < n), 1.0, 0.0)
        sel.append(jnp.where(jnp.sum(ahead, axis=0, keepdims=True) < MOBA_TOPK, 1.0, 0.0))
    return sel


def _attn_kernel(*refs, mode, n_blocks):
    if mode == "dilated":
        q_ref, k_ref, v_ref, bias_ref, mult_ref, o_ref, vt_ref = refs
    else:
        q_ref, k_ref, v_ref, o_ref, vt_ref, kmean_ref = refs
    tb = ATT_BLOCK
    i = pl.program_id(2)
    q2 = q_ref[...] * jnp.asarray(HEAD_DIM ** -0.5, BF16)
    lane = lax.broadcasted_iota(jnp.int32, q2.shape, 1)
    zero = jnp.zeros_like(q2)
    qm = (jnp.where(lane < HEAD_DIM, q2, zero), jnp.where(lane >= HEAD_DIM, q2, zero))

    @pl.when(i == 0)
    def _():
        for n in range(n_blocks):
            rows = slice(n * tb, (n + 1) * tb)
            vt_ref[:, rows] = v_ref[rows, :].astype(F32).T.astype(BF16)
        if mode == "moba":
            kmean_ref[...] = jnp.zeros_like(kmean_ref)
            for n in range(n_blocks):
                kb = k_ref[n * tb:(n + 1) * tb, :].astype(F32)
                kmean_ref[n:n + 1, :] = jnp.sum(kb, axis=0, keepdims=True) / tb

    if mode == "moba":
        key_pos = lax.broadcasted_iota(jnp.int32, (tb, tb), 0)
        qry_pos = lax.broadcasted_iota(jnp.int32, (tb, tb), 1)

    def query_block(ib):
        nk = (ib + 1) * tb
        k = k_ref[0:nk, :]
        ones_rows = jnp.ones((ONES_ROWS, nk), BF16)
        heads = []
        for h in range(2):
            s = _dot_nt(k, qm[h])
            if mode == "dilated":
                off = (n_blocks - 1 - ib) * tb
                s = s + bias_ref[off:off + nk, :]
            else:
                tiles = [s[j * tb:(j + 1) * tb, :] for j in range(ib + 1)]
                if ib > MOBA_TOPK:
                    sel = _moba_selection(kmean_ref[...], q2, h, ib)
                    tiles[:ib] = [jnp.where(sel[j] > 0.0, tiles[j], -jnp.inf) for j in range(ib)]
                tiles[ib] = jnp.where(key_pos <= qry_pos, tiles[ib], -jnp.inf)
                s = jnp.concatenate(tiles, axis=0)
            p = jnp.exp(s - jnp.max(s, axis=0, keepdims=True))
            if mode == "dilated":
                p = p * mult_ref[off:off + nk, :]
            vext = jnp.concatenate([vt_ref[h * HEAD_DIM:(h + 1) * HEAD_DIM, 0:nk], ones_rows], axis=0)
            acc = _dot(vext, p.astype(BF16))
            heads.append(acc[:HEAD_DIM, :] / acc[HEAD_DIM:HEAD_DIM + 1, :])
        o_ref[...] = jnp.concatenate(heads, axis=0).T.astype(o_ref.dtype)

    for ib in range(n_blocks):
        pl.when(i == ib)(functools.partial(query_block, ib))


def _dilated_tables(n_blocks):
    tb = ATT_BLOCK
    shape = (n_blocks, tb, tb)
    d = n_blocks - 1 - lax.broadcasted_iota(jnp.int32, shape, 0)
    c = lax.broadcasted_iota(jnp.int32, shape, 1)
    r = lax.broadcasted_iota(jnp.int32, shape, 2)
    delta = d * tb + r - c
    mult = jnp.zeros(shape, F32)
    for window, dilation in A_PATTERNS:
        hit = (delta >= 0) & (delta <= window) & (delta % dilation == 0)
        mult = mult + hit.astype(F32)
    bias = jnp.where(mult > 0, 0.0, -jnp.inf).astype(F32)
    return bias.reshape(n_blocks * tb, tb), mult.reshape(n_blocks * tb, tb)


def _attention(proj, q_col, mode, name):
    b, s, _ = proj.shape
    tb = ATT_BLOCK
    n_blocks = s // tb
    n_pairs = ATT_WIDTH // LANES
    k_col, v_col = q_col + n_pairs, q_col + 2 * n_pairs
    in_specs = [
        pl.BlockSpec((None, tb, LANES), lambda bi, hp, i: (bi, i, q_col + hp)),
        pl.BlockSpec((None, s, LANES), lambda bi, hp, i: (bi, 0, k_col + hp)),
        pl.BlockSpec((None, s, LANES), lambda bi, hp, i: (bi, 0, v_col + hp)),
    ]
    args = [proj, proj, proj]
    scratch = [pltpu.VMEM((LANES, s), BF16)]
    if mode == "dilated":
        bias, mult = _dilated_tables(n_blocks)
        table_spec = _resident_spec((s, tb), lambda bi, hp, i: (0, 0))
        in_specs += [table_spec, table_spec]
        args += [bias, mult]
    else:
        scratch += [pltpu.VMEM((ONES_ROWS, LANES), F32)]
    out = pl.pallas_call(
        functools.partial(_attn_kernel, mode=mode, n_blocks=n_blocks),
        grid=(b, n_pairs, n_blocks),
        in_specs=in_specs,
        out_specs=pl.BlockSpec((None, tb, LANES), lambda bi, hp, i: (bi, i, hp)),
        out_shape=jax.ShapeDtypeStruct((b, s, ATT_WIDTH), BF16),
        scratch_shapes=scratch,
        compiler_params=_params("parallel", "parallel", "arbitrary"),
        name=name,
    )(*args)
    return out.reshape(b * s, ATT_WIDTH)


def _gmlp_kernel(u_ref, v_ref, g_ref, b_ref, w_ref, bias_ref, o_ref, *, chunks):
    u = _gelu_tanh(u_ref[...].astype(F32))
    v = _layer_norm(_gelu_tanh(v_ref[...].astype(F32)), g_ref[...], b_ref[...])
    lane = lax.broadcasted_iota(jnp.int32, (C_CHUNK, LANES), 1)
    t_idx = lax.broadcasted_iota(jnp.int32, (C_CHUNK, C_CHUNK), 0)
    s_idx = lax.broadcasted_iota(jnp.int32, (C_CHUNK, C_CHUNK), 1)
    w = [jnp.where(s_idx <= t_idx, w_ref[g], 0.0).astype(BF16) for g in range(C_GROUPS)]
    per_tile = LANES // C_GROUP_DIM
    for ch in range(chunks):
        rows = slice(ch * C_CHUNK, (ch + 1) * C_CHUNK)
        for p in range(C_WIDTH // LANES):
            cols = slice(p * LANES, (p + 1) * LANES)
            v2 = v[rows, cols]
            sv = bias_ref[:, cols]
            for q in range(per_tile):
                in_group = (lane >= q * C_GROUP_DIM) & (lane < (q + 1) * C_GROUP_DIM)
                vq = jnp.where(in_group, v2, 0.0).astype(BF16)
                sv = sv + jnp.where(in_group, _dot(w[p * per_tile + q], vq), 0.0)
            o_ref[rows, cols] = (u[rows, cols] * sv).astype(o_ref.dtype)


def _gmlp(proj2d, u_col, v_col, ln_g, ln_b, w_s, b_s, tc):
    t = proj2d.shape[0]
    bias = jnp.repeat(b_s.T, C_GROUP_DIM, axis=1)
    const2 = lambda i: (0, 0)
    return pl.pallas_call(
        functools.partial(_gmlp_kernel, chunks=tc // C_CHUNK),
        grid=(t // tc,),
        in_specs=[pl.BlockSpec((tc, C_WIDTH), lambda i: (i, u_col)),
                  pl.BlockSpec((tc, C_WIDTH), lambda i: (i, v_col)),
                  pl.BlockSpec((1, C_WIDTH), const2),
                  pl.BlockSpec((1, C_WIDTH), const2),
                  pl.BlockSpec((C_GROUPS, C_CHUNK, C_CHUNK), lambda i: (0, 0, 0)),
                  pl.BlockSpec((C_CHUNK, C_WIDTH), const2)],
        out_specs=pl.BlockSpec((tc, C_WIDTH), lambda i: (i, 0)),
        out_shape=jax.ShapeDtypeStruct((t, C_WIDTH), BF16),
        compiler_params=_params("parallel"),
        name="gmlp",
    )(proj2d, proj2d, ln_g.reshape(1, -1), ln_b.reshape(1, -1), w_s, bias)


def _emit_norm(z, g_ref, b_ref, out_refs):
    y = _layer_norm(z, g_ref[...], b_ref[...])
    for o_ref in out_refs:
        o_ref[...] = y.astype(o_ref.dtype)


def _merge_kernel(oa_ref, ob_ref, yc_ref, ga_ref, gb_ref, gc_ref, x_ref,
                  wa_ref, wb_ref, wc_ref, wo_ref, g_ref, b_ref, *out_refs, alpha):
    def branch(y_ref, w_ref, gate_ref):
        gate = 0.5 * jnp.tanh(0.5 * gate_ref[...].astype(F32)) + 0.5
        return gate * _dot(y_ref[...], w_ref[...])

    merged = branch(oa_ref, wa_ref, ga_ref) + branch(ob_ref, wb_ref, gb_ref) + branch(yc_ref, wc_ref, gc_ref)
    mix = _dot(merged.astype(BF16), wo_ref[...])
    _emit_norm(alpha * x_ref[...] + mix, g_ref, b_ref, out_refs)


def _merge(oa, ob, yc, proj2d, gate_col, x, wa, wb, wc, wo, ln_g, ln_b, alpha, tm):
    t, d = x.shape
    row = lambda i: (i, 0)
    const = lambda i: (0, 0)
    branch_spec = pl.BlockSpec((tm, ATT_WIDTH), row)
    gate_specs = [pl.BlockSpec((tm, d), lambda i, c=gate_col + n: (i, c)) for n in range(3)]
    w_br_spec = _resident_spec((ATT_WIDTH, d), const)
    vec_spec = pl.BlockSpec((1, d), const)
    return pl.pallas_call(
        functools.partial(_merge_kernel, alpha=alpha),
        grid=(t // tm,),
        in_specs=[branch_spec, branch_spec, branch_spec, *gate_specs, pl.BlockSpec((tm, d), row),
                  w_br_spec, w_br_spec, w_br_spec, _resident_spec((d, d), const), vec_spec, vec_spec],
        out_specs=[pl.BlockSpec((tm, d), row), pl.BlockSpec((tm, d), row)],
        out_shape=[jax.ShapeDtypeStruct((t, d), F32), jax.ShapeDtypeStruct((t, d), BF16)],
        compiler_params=_params("parallel"),
        name="merge",
    )(oa, ob, yc, proj2d, proj2d, proj2d, x, wa, wb, wc, wo, ln_g.reshape(1, -1), ln_b.reshape(1, -1))


def _cross_kernel(x_ref, xb_ref, k_ref, v_ref, wq_ref, wo_ref, g_ref, b_ref, *out_refs, alpha):
    q = (_dot(xb_ref[...], wq_ref[...]) * (X_HEAD_DIM ** -0.5)).astype(BF16)
    heads = []
    for h in range(X_HEADS):
        cols = slice(h * X_HEAD_DIM, (h + 1) * X_HEAD_DIM)
        s = _dot_nt(q[:, cols], k_ref[:, cols])
        p = jnp.exp(s - jnp.max(s, axis=-1, keepdims=True))
        o = _dot(p.astype(BF16), v_ref[:, cols]) / jnp.sum(p, axis=-1, keepdims=True)
        heads.append(o.astype(BF16))
    cross = _dot(jnp.concatenate(heads, axis=-1), wo_ref[...])
    _emit_norm(alpha * x_ref[...] + cross, g_ref, b_ref, out_refs)


def _cross(x, xb, kv, wq, wo, ln_g, ln_b, alpha, batch, tm):
    t, d = x.shape
    s = t // batch
    m = kv.shape[1]
    x3, xb3 = x.reshape(batch, s, d), xb.reshape(batch, s, d)
    row = lambda bi, i: (bi, i, 0)
    const = lambda bi, i: (0, 0)
    outs = pl.pallas_call(
        functools.partial(_cross_kernel, alpha=alpha),
        grid=(batch, s // tm),
        in_specs=[pl.BlockSpec((None, tm, d), row), pl.BlockSpec((None, tm, d), row),
                  pl.BlockSpec((None, m, X_WIDTH), lambda bi, i: (bi, 0, 0)),
                  pl.BlockSpec((None, m, X_WIDTH), lambda bi, i: (bi, 0, 1)),
                  pl.BlockSpec((d, X_WIDTH), const), pl.BlockSpec((X_WIDTH, d), const),
                  pl.BlockSpec((1, d), const), pl.BlockSpec((1, d), const)],
        out_specs=[pl.BlockSpec((None, tm, d), row), pl.BlockSpec((None, tm, d), row)],
        out_shape=[jax.ShapeDtypeStruct((batch, s, d), F32), jax.ShapeDtypeStruct((batch, s, d), BF16)],
        compiler_params=_params("parallel", "parallel"),
        name="cross",
    )(x3, xb3, kv, kv, wq, wo, ln_g.reshape(1, -1), ln_b.reshape(1, -1))
    return outs[0].reshape(t, d), outs[1].reshape(t, d)


def _ffn_kernel(x_ref, xb_ref, w1_ref, w2_ref, g_ref, b_ref, *out_refs, alpha):
    acc_ref = out_refs[0]
    j = pl.program_id(1)

    @pl.when(j == 0)
    def _():
        acc_ref[...] = jnp.zeros_like(acc_ref)

    h = jnp.maximum(_dot(xb_ref[...], w1_ref[...]), 0.0)
    acc_ref[...] += _dot((h * h).astype(BF16), w2_ref[...])

    @pl.when(j == pl.num_programs(1) - 1)
    def _():
        _emit_norm(alpha * x_ref[...] + acc_ref[...], g_ref, b_ref, out_refs)


def _ffn(x, xb, w1, w2, ln_g, ln_b, alpha, tm, tf, with_bf16):
    t, d = x.shape
    d_ff = w1.shape[1]
    row = lambda i, j: (i, 0)
    const = lambda i, j: (0, 0)
    out_dtypes = [F32, BF16] if with_bf16 else [F32]
    return pl.pallas_call(
        functools.partial(_ffn_kernel, alpha=alpha),
        grid=(t // tm, d_ff // tf),
        in_specs=[pl.BlockSpec((tm, d), row), pl.BlockSpec((tm, d), row),
                  pl.BlockSpec((d, tf), lambda i, j: (0, j)), pl.BlockSpec((tf, d), lambda i, j: (j, 0)),
                  pl.BlockSpec((1, d), const), pl.BlockSpec((1, d), const)],
        out_specs=[pl.BlockSpec((tm, d), row) for _ in out_dtypes],
        out_shape=[jax.ShapeDtypeStruct((t, d), dt) for dt in out_dtypes],
        compiler_params=_params("parallel", "arbitrary"),
        name="ffn",
    )(x, xb, w1, w2, ln_g.reshape(1, -1), ln_b.reshape(1, -1))


def kernel(x, mem, w_in, w_br_a, w_br_b, w_br_c, w_out, c_ln_g, c_ln_b, c_ws, c_bs,
           ln1_g, ln1_b, w_xq, w_xk, w_xv, w_xo, ln2_g, ln2_b, w_ff1, w_ff2, ln3_g, ln3_b):
    batch, seq, d = x.shape
    depth = w_in.shape[0]
    t = batch * seq
    alpha = (2 * depth) ** 0.25
    assert all(seq % window == 0 for window, _ in A_PATTERNS) and seq % ATT_BLOCK == 0
    assert w_in.shape[2] == 3 * ATT_WIDTH * 2 + 2 * C_WIDTH + 3 * d

    qa_blk, qb_blk = 0, 3 * ATT_WIDTH // LANES
    u_blk, v_blk = 6, 7
    gate_blk = (6 * ATT_WIDTH + 2 * C_WIDTH) // d

    xf = x.reshape(t, d)
    xb = xf.astype(BF16)
    mem_b = mem.reshape(-1, d).astype(BF16)
    for l in range(depth):
        proj2d = _matmul(xb, w_in[l].astype(BF16), BF16, 1024, 1024, "in_proj")
        proj = proj2d.reshape(batch, seq, -1)
        oa = _attention(proj, qa_blk, "dilated", "dilated_attn")
        ob = _attention(proj, qb_blk, "moba", "moba_attn")
        yc = _gmlp(proj2d, u_blk, v_blk, c_ln_g[l], c_ln_b[l], c_ws[l], c_bs[l], 512)
        xf, xb = _merge(oa, ob, yc, proj2d, gate_blk, xf,
                        w_br_a[l].astype(BF16), w_br_b[l].astype(BF16), w_br_c[l].astype(BF16),
                        w_out[l].astype(BF16), ln1_g[l], ln1_b[l], alpha, 256)
        w_kv = jnp.concatenate([w_xk[l], w_xv[l]], axis=1).astype(BF16)
        kv = _matmul(mem_b, w_kv, BF16, 1024, 1024, "mem_proj").reshape(batch, -1, 2 * X_WIDTH)
        xf, xb = _cross(xf, xb, kv, w_xq[l].astype(BF16), w_xo[l].astype(BF16),
                        ln2_g[l], ln2_b[l], alpha, batch, 512)
        last = l == depth - 1
        outs = _ffn(xf, xb, w_ff1[l].astype(BF16), w_ff2[l].astype(BF16),
                    ln3_g[l], ln3_b[l], alpha, 512, 1024, not last)
        xf = outs[0]
        xb = None if last else outs[1]
    return xf.reshape(batch, seq, d)
```

```python
import functools

import jax
import jax.numpy as jnp
from jax import lax
from jax.experimental import pallas as pl
from jax.experimental.pallas import tpu as pltpu

F32 = jnp.float32
BF16 = jnp.bfloat16

HEAD_DIM = 64
A_PATTERNS = ((128, 1), (512, 4), (2048, 16))
ATT_WIDTH = 512
MOBA_BLOCK = 256
MOBA_TOPK = 3
C_GROUPS = 8
C_GROUP_DIM = 64
C_WIDTH = C_GROUPS * C_GROUP_DIM
C_CHUNK = 128
X_HEADS = 4
X_HEAD_DIM = 128
X_WIDTH = X_HEADS * X_HEAD_DIM
LN_EPS = 1e-5

LANES = 128
ATT_BLOCK = 256
BF16_SUBLANES = 16
ONES_ROWS = BF16_SUBLANES
VMEM_LIMIT = 56 * 1024 * 1024


def _params(*sem):
    return pltpu.CompilerParams(dimension_semantics=sem, vmem_limit_bytes=VMEM_LIMIT)


def _resident_spec(shape, index_map):
    return pl.BlockSpec(shape, index_map, pipeline_mode=pl.Buffered(1))


def _layer_norm(z, g, b):
    mu = jnp.mean(z, axis=-1, keepdims=True)
    zc = z - mu
    var = jnp.mean(zc * zc, axis=-1, keepdims=True)
    return zc * lax.rsqrt(var + LN_EPS) * g + b


def _gelu_tanh(x):
    c = (2.0 / jnp.pi) ** 0.5
    return 0.5 * x * (1.0 + jnp.tanh(c * (x + 0.044715 * (x * x * x))))


def _dot(a, b):
    return jnp.dot(a, b, preferred_element_type=F32)


def _dot_nt(a, b):
    return lax.dot_general(a, b, (((1,), (1,)), ((), ())), preferred_element_type=F32)


def _matmul_kernel(a_ref, w_ref, o_ref):
    o_ref[...] = _dot(a_ref[...], w_ref[...]).astype(o_ref.dtype)


def _matmul(a, w, out_dtype, tm, tn, name):
    m, k = a.shape
    n = w.shape[1]
    tm, tn = min(tm, m), min(tn, n)
    assert m % tm == 0 and n % tn == 0
    return pl.pallas_call(
        _matmul_kernel,
        grid=(m // tm, n // tn),
        in_specs=[pl.BlockSpec((tm, k), lambda i, j: (i, 0)),
                  pl.BlockSpec((k, tn), lambda i, j: (0, j))],
        out_specs=pl.BlockSpec((tm, tn), lambda i, j: (i, j)),
        out_shape=jax.ShapeDtypeStruct((m, n), out_dtype),
        compiler_params=_params("parallel", "parallel"),
        name=name,
    )(a, w)


def _matmul_f32w_kernel(a_ref, w_ref, o_ref, wb_ref):
    @pl.when(pl.program_id(1) == 0)
    def _():
        wb_ref[...] = w_ref[...].astype(BF16)

    o_ref[...] = _dot(a_ref[...], wb_ref[...]).astype(o_ref.dtype)


def _matmul_f32w(a, w_stack, layer, out_dtype, tm, tn, name):
    m, k = a.shape
    n = w_stack.shape[2]
    assert m % tm == 0 and n % tn == 0
    return pl.pallas_call(
        _matmul_f32w_kernel,
        grid=(n // tn, m // tm),
        in_specs=[pl.BlockSpec((tm, k), lambda j, i: (i, 0)),
                  pl.BlockSpec((None, k, tn), lambda j, i: (layer, 0, j))],
        out_specs=pl.BlockSpec((tm, tn), lambda j, i: (i, j)),
        out_shape=jax.ShapeDtypeStruct((m, n), out_dtype),
        scratch_shapes=[pltpu.VMEM((k, tn), BF16)],
        compiler_params=_params("parallel", "arbitrary"),
        name=name,
    )(a, w_stack)


def _moba_selection(kmean, q2, head, n_past):
    klane = lax.broadcasted_iota(jnp.int32, kmean.shape, 1)
    km = jnp.where((klane < HEAD_DIM) == (head == 0), kmean, 0.0)
    km_hi = km.astype(BF16)
    km_lo = (km - km_hi.astype(F32)).astype(BF16)
    gate = _dot_nt(km_hi, q2) + _dot_nt(km_lo, q2)
    row = lax.broadcasted_iota(jnp.int32, gate.shape, 0)
    gate = jnp.where(row < n_past, gate, -jnp.inf)
    sel = []
    for n in range(n_past):
        gn = gate[n:n + 1, :]
        ahead = jnp.where(gate > gn, 1.0, 0.0) + jnp.where((gate == gn) & (row < n), 1.0, 0.0)
        sel.append(jnp.where(jnp.sum(ahead, axis=0, keepdims=True) < MOBA_TOPK, 1.0, 0.0))
    return sel


def _attn_kernel(*refs, mode, n_blocks):
    if mode == "dilated":
        q_ref, k_ref, v_ref, w_ref, bias_ref, o_ref, wb_ref, vt_ref = refs
    else:
        q_ref, k_ref, v_ref, w_ref, o_ref, wb_ref, vt_ref, kmean_ref = refs
    wb_ref[...] = w_ref[...].astype(BF16)
    tb = ATT_BLOCK
    i = pl.program_id(2)
    q2 = q_ref[...] * jnp.asarray(HEAD_DIM ** -0.5, BF16)
    lane = lax.broadcasted_iota(jnp.int32, q2.shape, 1)
    zero = jnp.zeros_like(q2)
    qm = (jnp.where(lane < HEAD_DIM, q2, zero), jnp.where(lane >= HEAD_DIM, q2, zero))

    @pl.when(i == 0)
    def _():
        for n in range(n_blocks):
            rows = slice(n * tb, (n + 1) * tb)
            vt_ref[:, rows] = v_ref[rows, :].astype(F32).T.astype(BF16)
        if mode == "moba":
            kmean_ref[...] = jnp.zeros_like(kmean_ref)
            for n in range(n_blocks):
                kb = k_ref[n * tb:(n + 1) * tb, :].astype(F32)
                kmean_ref[n:n + 1, :] = jnp.sum(kb, axis=0, keepdims=True) / tb

    if mode == "moba":
        key_pos = lax.broadcasted_iota(jnp.int32, (tb, tb), 0)
        qry_pos = lax.broadcasted_iota(jnp.int32, (tb, tb), 1)

    ones_rows = jnp.ones((ONES_ROWS, tb), BF16)

    def query_block(ib):
        blocks = [slice(j * tb, (j + 1) * tb) for j in range(ib + 1)]
        sel = [None, None]
        if mode == "moba" and ib > MOBA_TOPK:
            sel = [_moba_selection(kmean_ref[...], q2, h, ib) for h in range(2)]

        def score(h, j):
            return _dot_nt(k_ref[blocks[j], :], qm[h])

        def piece(h, j, s):
            if mode == "dilated":
                off = (n_blocks - 1 - ib + j) * tb
                s = s + bias_ref[off:off + tb, :]
            elif j == ib:
                s = jnp.where(key_pos <= qry_pos, s, -jnp.inf)
            elif sel[h] is not None:
                s = jnp.where(sel[h][j] > 0.0, s, -jnp.inf)
            m = jnp.max(s, axis=0, keepdims=True)
            p = jnp.exp(s - jnp.where(m == -jnp.inf, 0.0, m))
            return m, p.astype(BF16)

        def weighted(h, j, p):
            vext = jnp.concatenate([vt_ref[h * HEAD_DIM:(h + 1) * HEAD_DIM, blocks[j]], ones_rows], axis=0)
            return _dot(vext, p)

        n = ib + 1
        s0 = [score(0, j) for j in range(n)]
        pieces0, s1 = [], []
        for j in range(n):
            s1.append(score(1, j))
            pieces0.append(piece(0, j, s0[j]))
        pieces1, pv0 = [], []
        for j in range(n):
            pv0.append(weighted(0, j, pieces0[j][1]))
            pieces1.append(piece(1, j, s1[j]))
        pv1 = [weighted(1, j, pieces1[j][1]) for j in range(n)]

        heads = []
        for pieces, pvs in ((pieces0, pv0), (pieces1, pv1)):
            m_all = functools.reduce(jnp.maximum, [m for m, _ in pieces])
            acc = sum(jnp.exp(m - m_all) * pv for (m, _), pv in zip(pieces, pvs))
            heads.append(acc[:HEAD_DIM, :] / acc[HEAD_DIM:HEAD_DIM + 1, :])
        o_ref[...] = jnp.concatenate(heads, axis=0).T.astype(o_ref.dtype)

    for ib in range(n_blocks):
        pl.when(i == ib)(functools.partial(query_block, ib))


def _dilated_bias(n_blocks):
    tb = ATT_BLOCK
    shape = (n_blocks, tb, tb)
    d = n_blocks - 1 - lax.broadcasted_iota(jnp.int32, shape, 0)
    c = lax.broadcasted_iota(jnp.int32, shape, 1)
    r = lax.broadcasted_iota(jnp.int32, shape, 2)
    delta = d * tb + r - c
    mult = jnp.zeros(shape, F32)
    for window, dilation in A_PATTERNS:
        hit = (delta >= 0) & (delta <= window) & (delta % dilation == 0)
        mult = mult + hit.astype(F32)
    return jnp.log(mult).reshape(n_blocks * tb, tb)


def _attention(proj, q_col, mode, w_stack, layer, name):
    b, s, _ = proj.shape
    tb = ATT_BLOCK
    n_blocks = s // tb
    n_pairs = ATT_WIDTH // LANES
    k_col, v_col = q_col + n_pairs, q_col + 2 * n_pairs
    w_rows, w_cols = w_stack.shape[1:]
    n_slabs = b * n_pairs * n_blocks // 2
    slab = w_rows // n_slabs
    assert slab * n_slabs == w_rows and slab % BF16_SUBLANES == 0
    slab_index = lambda bi, hp, i: ((bi * n_pairs + hp) * n_blocks + i) // 2
    in_specs = [
        pl.BlockSpec((None, tb, LANES), lambda bi, hp, i: (bi, i, q_col + hp)),
        pl.BlockSpec((None, s, LANES), lambda bi, hp, i: (bi, 0, k_col + hp)),
        pl.BlockSpec((None, s, LANES), lambda bi, hp, i: (bi, 0, v_col + hp)),
        pl.BlockSpec((None, slab, w_cols), lambda bi, hp, i: (layer, slab_index(bi, hp, i), 0)),
    ]
    args = [proj, proj, proj, w_stack]
    scratch = [pltpu.VMEM((LANES, s), BF16)]
    if mode == "dilated":
        in_specs += [_resident_spec((s, tb), lambda bi, hp, i: (0, 0))]
        args += [_dilated_bias(n_blocks)]
    else:
        scratch += [pltpu.VMEM((ONES_ROWS, LANES), F32)]
    out, w_bf16 = pl.pallas_call(
        functools.partial(_attn_kernel, mode=mode, n_blocks=n_blocks),
        grid=(b, n_pairs, n_blocks),
        in_specs=in_specs,
        out_specs=[pl.BlockSpec((None, tb, LANES), lambda bi, hp, i: (bi, i, hp)),
                   pl.BlockSpec((slab, w_cols), lambda bi, hp, i: (slab_index(bi, hp, i), 0))],
        out_shape=[jax.ShapeDtypeStruct((b, s, ATT_WIDTH), BF16),
                   jax.ShapeDtypeStruct((w_rows, w_cols), BF16)],
        scratch_shapes=scratch,
        compiler_params=_params("arbitrary", "arbitrary", "arbitrary"),
        name=name,
    )(*args)
    return out.reshape(b * s, ATT_WIDTH), w_bf16


def _gmlp_kernel(u_ref, v_ref, g_ref, b_ref, w_ref, bias_ref, o_ref, *, chunks):
    u = _gelu_tanh(u_ref[...].astype(F32))
    v = _layer_norm(_gelu_tanh(v_ref[...].astype(F32)), g_ref[...], b_ref[...])
    lane = lax.broadcasted_iota(jnp.int32, (C_CHUNK, LANES), 1)
    t_idx = lax.broadcasted_iota(jnp.int32, (C_CHUNK, C_CHUNK), 0)
    s_idx = lax.broadcasted_iota(jnp.int32, (C_CHUNK, C_CHUNK), 1)
    w = [jnp.where(s_idx <= t_idx, w_ref[g], 0.0).astype(BF16) for g in range(C_GROUPS)]
    per_tile = LANES // C_GROUP_DIM
    for ch in range(chunks):
        rows = slice(ch * C_CHUNK, (ch + 1) * C_CHUNK)
        for p in range(C_WIDTH // LANES):
            cols = slice(p * LANES, (p + 1) * LANES)
            v2 = v[rows, cols]
            sv = bias_ref[:, cols]
            for q in range(per_tile):
                in_group = (lane >= q * C_GROUP_DIM) & (lane < (q + 1) * C_GROUP_DIM)
                vq = jnp.where(in_group, v2, 0.0).astype(BF16)
                sv = sv + jnp.where(in_group, _dot(w[p * per_tile + q], vq), 0.0)
            o_ref[rows, cols] = (u[rows, cols] * sv).astype(o_ref.dtype)


def _gmlp(proj2d, u_col, v_col, ln_g, ln_b, w_s, b_s, tc):
    t = proj2d.shape[0]
    bias = jnp.repeat(b_s.T, C_GROUP_DIM, axis=1)
    const2 = lambda i: (0, 0)
    return pl.pallas_call(
        functools.partial(_gmlp_kernel, chunks=tc // C_CHUNK),
        grid=(t // tc,),
        in_specs=[pl.BlockSpec((tc, C_WIDTH), lambda i: (i, u_col)),
                  pl.BlockSpec((tc, C_WIDTH), lambda i: (i, v_col)),
                  pl.BlockSpec((1, C_WIDTH), const2),
                  pl.BlockSpec((1, C_WIDTH), const2),
                  pl.BlockSpec((C_GROUPS, C_CHUNK, C_CHUNK), lambda i: (0, 0, 0)),
                  pl.BlockSpec((C_CHUNK, C_WIDTH), const2)],
        out_specs=pl.BlockSpec((tc, C_WIDTH), lambda i: (i, 0)),
        out_shape=jax.ShapeDtypeStruct((t, C_WIDTH), BF16),
        compiler_params=_params("parallel"),
        name="gmlp",
    )(proj2d, proj2d, ln_g.reshape(1, -1), ln_b.reshape(1, -1), w_s, bias)


def _emit_norm(z, g_ref, b_ref, out_refs):
    y = _layer_norm(z, g_ref[...], b_ref[...])
    for o_ref in out_refs:
        o_ref[...] = y.astype(o_ref.dtype)


def _merge_kernel(oa_ref, ob_ref, yc_ref, ga_ref, gb_ref, gc_ref, x_ref,
                  wa_ref, wb_ref, wc_ref, wo_ref, g_ref, b_ref, *out_refs, alpha):
    def branch(y_ref, w_ref, gate_ref):
        gate = 0.5 * jnp.tanh(0.5 * gate_ref[...].astype(F32)) + 0.5
        return gate * _dot(y_ref[...], w_ref[...])

    merged = branch(oa_ref, wa_ref, ga_ref) + branch(ob_ref, wb_ref, gb_ref) + branch(yc_ref, wc_ref, gc_ref)
    mix = _dot(merged.astype(BF16), wo_ref[...])
    _emit_norm(alpha * x_ref[...] + mix, g_ref, b_ref, out_refs)


def _merge(oa, ob, yc, proj2d, gate_col, x, wa, wb, wc, wo, ln_g, ln_b, alpha, tm):
    t, d = x.shape
    row = lambda i: (i, 0)
    const = lambda i: (0, 0)
    branch_spec = pl.BlockSpec((tm, ATT_WIDTH), row)
    gate_specs = [pl.BlockSpec((tm, d), lambda i, c=gate_col + n: (i, c)) for n in range(3)]
    w_br_spec = _resident_spec((ATT_WIDTH, d), const)
    vec_spec = pl.BlockSpec((1, d), const)
    return pl.pallas_call(
        functools.partial(_merge_kernel, alpha=alpha),
        grid=(t // tm,),
        in_specs=[branch_spec, branch_spec, branch_spec, *gate_specs, pl.BlockSpec((tm, d), row),
                  w_br_spec, w_br_spec, w_br_spec, _resident_spec((d, d), const), vec_spec, vec_spec],
        out_specs=[pl.BlockSpec((tm, d), row), pl.BlockSpec((tm, d), row)],
        out_shape=[jax.ShapeDtypeStruct((t, d), F32), jax.ShapeDtypeStruct((t, d), BF16)],
        compiler_params=_params("parallel"),
        name="merge",
    )(oa, ob, yc, proj2d, proj2d, proj2d, x, wa, wb, wc, wo, ln_g.reshape(1, -1), ln_b.reshape(1, -1))


def _cross_kernel(x_ref, xb_ref, k_ref, v_ref, wq_ref, wo_ref, g_ref, b_ref, *out_refs, alpha):
    q = (_dot(xb_ref[...], wq_ref[...]) * (X_HEAD_DIM ** -0.5)).astype(BF16)
    heads = []
    for h in range(X_HEADS):
        cols = slice(h * X_HEAD_DIM, (h + 1) * X_HEAD_DIM)
        s = _dot_nt(q[:, cols], k_ref[:, cols])
        p = jnp.exp(s - jnp.max(s, axis=-1, keepdims=True))
        o = _dot(p.astype(BF16), v_ref[:, cols]) / jnp.sum(p, axis=-1, keepdims=True)
        heads.append(o.astype(BF16))
    cross = _dot(jnp.concatenate(heads, axis=-1), wo_ref[...])
    _emit_norm(alpha * x_ref[...] + cross, g_ref, b_ref, out_refs)


def _cross(x, xb, kv, wq, wo, ln_g, ln_b, alpha, batch, tm):
    t, d = x.shape
    s = t // batch
    m = kv.shape[1]
    x3, xb3 = x.reshape(batch, s, d), xb.reshape(batch, s, d)
    row = lambda bi, i: (bi, i, 0)
    const = lambda bi, i: (0, 0)
    outs = pl.pallas_call(
        functools.partial(_cross_kernel, alpha=alpha),
        grid=(batch, s // tm),
        in_specs=[pl.BlockSpec((None, tm, d), row), pl.BlockSpec((None, tm, d), row),
                  pl.BlockSpec((None, m, X_WIDTH), lambda bi, i: (bi, 0, 0)),
                  pl.BlockSpec((None, m, X_WIDTH), lambda bi, i: (bi, 0, 1)),
                  pl.BlockSpec((d, X_WIDTH), const), pl.BlockSpec((X_WIDTH, d), const),
                  pl.BlockSpec((1, d), const), pl.BlockSpec((1, d), const)],
        out_specs=[pl.BlockSpec((None, tm, d), row), pl.BlockSpec((None, tm, d), row)],
        out_shape=[jax.ShapeDtypeStruct((batch, s, d), F32), jax.ShapeDtypeStruct((batch, s, d), BF16)],
        compiler_params=_params("parallel", "parallel"),
        name="cross",
    )(x3, xb3, kv, kv, wq, wo, ln_g.reshape(1, -1), ln_b.reshape(1, -1))
    return outs[0].reshape(t, d), outs[1].reshape(t, d)


def _ffn_kernel(x_ref, xb_ref, w1_ref, w2_ref, g_ref, b_ref, *out_refs, alpha):
    acc_ref = out_refs[0]
    j = pl.program_id(1)

    @pl.when(j == 0)
    def _():
        acc_ref[...] = jnp.zeros_like(acc_ref)

    h = jnp.maximum(_dot(xb_ref[...], w1_ref[...]), 0.0)
    acc_ref[...] += _dot((h * h).astype(BF16), w2_ref[...])

    @pl.when(j == pl.num_programs(1) - 1)
    def _():
        _emit_norm(alpha * x_ref[...] + acc_ref[...], g_ref, b_ref, out_refs)


def _ffn(x, xb, w1, w2, ln_g, ln_b, alpha, tm, tf, with_bf16):
    t, d = x.shape
    d_ff = w1.shape[1]
    row = lambda i, j: (i, 0)
    const = lambda i, j: (0, 0)
    out_dtypes = [F32, BF16] if with_bf16 else [F32]
    return pl.pallas_call(
        functools.partial(_ffn_kernel, alpha=alpha),
        grid=(t // tm, d_ff // tf),
        in_specs=[pl.BlockSpec((tm, d), row), pl.BlockSpec((tm, d), row),
                  pl.BlockSpec((d, tf), lambda i, j: (0, j)), pl.BlockSpec((tf, d), lambda i, j: (j, 0)),
                  pl.BlockSpec((1, d), const), pl.BlockSpec((1, d), const)],
        out_specs=[pl.BlockSpec((tm, d), row) for _ in out_dtypes],
        out_shape=[jax.ShapeDtypeStruct((t, d), dt) for dt in out_dtypes],
        compiler_params=_params("parallel", "arbitrary"),
        name="ffn",
    )(x, xb, w1, w2, ln_g.reshape(1, -1), ln_b.reshape(1, -1))


def kernel(x, mem, w_in, w_br_a, w_br_b, w_br_c, w_out, c_ln_g, c_ln_b, c_ws, c_bs,
           ln1_g, ln1_b, w_xq, w_xk, w_xv, w_xo, ln2_g, ln2_b, w_ff1, w_ff2, ln3_g, ln3_b):
    batch, seq, d = x.shape
    depth = w_in.shape[0]
    t = batch * seq
    alpha = (2 * depth) ** 0.25
    assert all(seq % window == 0 for window, _ in A_PATTERNS) and seq % ATT_BLOCK == 0
    assert w_in.shape[2] == 3 * ATT_WIDTH * 2 + 2 * C_WIDTH + 3 * d

    qa_blk, qb_blk = 0, 3 * ATT_WIDTH // LANES
    u_blk, v_blk = 6, 7
    gate_blk = (6 * ATT_WIDTH + 2 * C_WIDTH) // d

    xf = x.reshape(t, d)
    xb = xf.astype(BF16)
    mem_b = mem.reshape(-1, d).astype(BF16)
    for l in range(depth):
        proj2d = _matmul_f32w(xb, w_in, l, BF16, 1024, 1024, "in_proj")
        proj = proj2d.reshape(batch, seq, -1)
        oa, w1 = _attention(proj, qa_blk, "dilated", w_ff1, l, "dilated_attn")
        ob, w2 = _attention(proj, qb_blk, "moba", w_ff2, l, "moba_attn")
        yc = _gmlp(proj2d, u_blk, v_blk, c_ln_g[l], c_ln_b[l], c_ws[l], c_bs[l], 512)
        xf, xb = _merge(oa, ob, yc, proj2d, gate_blk, xf,
                        w_br_a[l].astype(BF16), w_br_b[l].astype(BF16), w_br_c[l].astype(BF16),
                        w_out[l].astype(BF16), ln1_g[l], ln1_b[l], alpha, 256)
        w_kv = jnp.concatenate([w_xk[l], w_xv[l]], axis=1).astype(BF16)
        kv = _matmul(mem_b, w_kv, BF16, 1024, 1024, "mem_proj").reshape(batch, -1, 2 * X_WIDTH)
        xf, xb = _cross(xf, xb, kv, w_xq[l].astype(BF16), w_xo[l].astype(BF16),
                        ln2_g[l], ln2_b[l], alpha, batch, 512)
        last = l == depth - 1
        outs = _ffn(xf, xb, w1, w2,
                    ln3_g[l], ln3_b[l], alpha, 512, 1024, not last)
        xf = outs[0]
        xb = None if last else outs[1]
    return xf.reshape(batch, seq, d)
```

```python
import functools

import jax
import jax.numpy as jnp
from jax import lax
from jax.experimental import pallas as pl
from jax.experimental.pallas import tpu as pltpu

F32 = jnp.float32
BF16 = jnp.bfloat16

HEAD_DIM = 64
A_PATTERNS = ((128, 1), (512, 4), (2048, 16))
ATT_WIDTH = 512
MOBA_BLOCK = 256
MOBA_TOPK = 3
C_GROUPS = 8
C_GROUP_DIM = 64
C_WIDTH = C_GROUPS * C_GROUP_DIM
C_CHUNK = 128
X_HEADS = 4
X_HEAD_DIM = 128
X_WIDTH = X_HEADS * X_HEAD_DIM
LN_EPS = 1e-5

LANES = 128
ATT_BLOCK = 256
ATT_HEADS_PER_STEP = 8
BF16_SUBLANES = 16
ONES_ROWS = BF16_SUBLANES
VMEM_LIMIT = 56 * 1024 * 1024


def _params(*sem):
    return pltpu.CompilerParams(dimension_semantics=sem, vmem_limit_bytes=VMEM_LIMIT)


def _resident_spec(shape, index_map):
    return pl.BlockSpec(shape, index_map, pipeline_mode=pl.Buffered(1))


def _layer_norm(z, g, b):
    mu = jnp.mean(z, axis=-1, keepdims=True)
    zc = z - mu
    var = jnp.mean(zc * zc, axis=-1, keepdims=True)
    return zc * lax.rsqrt(var + LN_EPS) * g + b


def _gelu_tanh(x):
    c = (2.0 / jnp.pi) ** 0.5
    return 0.5 * x * (1.0 + jnp.tanh(c * (x + 0.044715 * (x * x * x))))


def _dot(a, b):
    return jnp.dot(a, b, preferred_element_type=F32)


def _dot_nt(a, b):
    return lax.dot_general(a, b, (((1,), (1,)), ((), ())), preferred_element_type=F32)


def _matmul_kernel(a_ref, w_ref, o_ref):
    o_ref[...] = _dot(a_ref[...], w_ref[...]).astype(o_ref.dtype)


def _matmul(a, w, out_dtype, tm, tn, name):
    m, k = a.shape
    n = w.shape[1]
    tm, tn = min(tm, m), min(tn, n)
    assert m % tm == 0 and n % tn == 0
    return pl.pallas_call(
        _matmul_kernel,
        grid=(m // tm, n // tn),
        in_specs=[pl.BlockSpec((tm, k), lambda i, j: (i, 0)),
                  pl.BlockSpec((k, tn), lambda i, j: (0, j))],
        out_specs=pl.BlockSpec((tm, tn), lambda i, j: (i, j)),
        out_shape=jax.ShapeDtypeStruct((m, n), out_dtype),
        compiler_params=_params("parallel", "parallel"),
        name=name,
    )(a, w)


def _matmul_f32w_kernel(a_ref, w_ref, o_ref, wb_ref):
    @pl.when(pl.program_id(1) == 0)
    def _():
        wb_ref[...] = w_ref[...].astype(BF16)

    o_ref[...] = _dot(a_ref[...].astype(BF16), wb_ref[...]).astype(o_ref.dtype)


def _matmul_f32w(a, w_stack, layer, out_dtype, tm, tn, name):
    m, k = a.shape
    n = w_stack.shape[2]
    assert m % tm == 0 and n % tn == 0
    return pl.pallas_call(
        _matmul_f32w_kernel,
        grid=(n // tn, m // tm),
        in_specs=[pl.BlockSpec((tm, k), lambda j, i: (i, 0)),
                  pl.BlockSpec((None, k, tn), lambda j, i: (layer, 0, j))],
        out_specs=pl.BlockSpec((tm, tn), lambda j, i: (i, j)),
        out_shape=jax.ShapeDtypeStruct((m, n), out_dtype),
        scratch_shapes=[pltpu.VMEM((k, tn), BF16)],
        compiler_params=_params("parallel", "arbitrary"),
        name=name,
    )(a, w_stack)


def _moba_selection(kmean, q2, head, n_past):
    klane = lax.broadcasted_iota(jnp.int32, kmean.shape, 1)
    km = jnp.where((klane < HEAD_DIM) == (head == 0), kmean, 0.0)
    km_hi = km.astype(BF16)
    km_lo = (km - km_hi.astype(F32)).astype(BF16)
    gate = _dot_nt(km_hi, q2) + _dot_nt(km_lo, q2)
    row = lax.broadcasted_iota(jnp.int32, gate.shape, 0)
    gate = jnp.where(row < n_past, gate, -jnp.inf)
    sel = []
    for n in range(n_past):
        gn = gate[n:n + 1, :]
        ahead = jnp.where(gate > gn, 1.0, 0.0) + jnp.where((gate == gn) & (row < n), 1.0, 0.0)
        sel.append(jnp.where(jnp.sum(ahead, axis=0, keepdims=True) < MOBA_TOPK, 1.0, 0.0))
    return sel


def _attn_kernel(*refs, mode, n_blocks):
    if mode == "dilated":
        q_ref, k_ref, v_ref, w_ref, bias_ref, o_ref, wb_ref, vt_ref = refs
    else:
        q_ref, k_ref, v_ref, w_ref, o_ref, wb_ref, vt_ref, kmean_ref = refs
    wb_ref[...] = w_ref[...].astype(BF16)
    tb = ATT_BLOCK
    n_heads = q_ref.shape[1] // HEAD_DIM
    i = pl.program_id(2)
    lane = lax.broadcasted_iota(jnp.int32, (tb, LANES), 1)
    q2, qm = [], []
    for t in range(n_heads // 2):
        qt = q_ref[:, t * LANES:(t + 1) * LANES] * jnp.asarray(HEAD_DIM ** -0.5, BF16)
        zero = jnp.zeros_like(qt)
        q2.append(qt)
        qm += [jnp.where(lane < HEAD_DIM, qt, zero), jnp.where(lane >= HEAD_DIM, qt, zero)]

    @pl.when(i == 0)
    def _():
        for t in range(n_heads // 2):
            cols = slice(t * LANES, (t + 1) * LANES)
            for n in range(n_blocks):
                rows = slice(n * tb, (n + 1) * tb)
                vt_ref[cols, rows] = v_ref[rows, cols].astype(F32).T.astype(BF16)
        if mode == "moba":
            kmean_ref[...] = jnp.zeros_like(kmean_ref)
            for n in range(n_blocks):
                kb = k_ref[n * tb:(n + 1) * tb, :].astype(F32)
                kmean_ref[n:n + 1, :] = jnp.sum(kb, axis=0, keepdims=True) / tb

    if mode == "moba":
        key_pos = lax.broadcasted_iota(jnp.int32, (tb, tb), 0)
        qry_pos = lax.broadcasted_iota(jnp.int32, (tb, tb), 1)

    ones_rows = jnp.ones((ONES_ROWS, tb), BF16)

    def query_block(ib):
        blocks = [slice(j * tb, (j + 1) * tb) for j in range(ib + 1)]
        tile = lambda h: slice(h // 2 * LANES, (h // 2 + 1) * LANES)
        sel = [None] * n_heads
        if mode == "moba" and ib > MOBA_TOPK:
            sel = [_moba_selection(kmean_ref[:, tile(h)], q2[h // 2], h % 2, ib) for h in range(n_heads)]

        def score(h, j):
            return _dot_nt(k_ref[blocks[j], tile(h)], qm[h])

        def piece(h, j, s):
            if mode == "dilated":
                off = (n_blocks - 1 - ib + j) * tb
                s = s + bias_ref[off:off + tb, :]
            elif j == ib:
                s = jnp.where(key_pos <= qry_pos, s, -jnp.inf)
            elif sel[h] is not None:
                s = jnp.where(sel[h][j] > 0.0, s, -jnp.inf)
            m = jnp.max(s, axis=0, keepdims=True)
            p = jnp.exp(s - jnp.where(m == -jnp.inf, 0.0, m))
            return m, p.astype(BF16)

        def weighted(h, j, p):
            vext = jnp.concatenate([vt_ref[h * HEAD_DIM:(h + 1) * HEAD_DIM, blocks[j]], ones_rows], axis=0)
            return _dot(vext, p)

        n = ib + 1
        scores = [[score(0, j) for j in range(n)]] + [[] for _ in range(n_heads - 1)]
        pieces = [[] for _ in range(n_heads)]
        pvs = [[] for _ in range(n_heads)]
        for h in range(n_heads + 1):
            for j in range(n):
                if h + 1 < n_heads:
                    scores[h + 1].append(score(h + 1, j))
                if h >= 1:
                    pvs[h - 1].append(weighted(h - 1, j, pieces[h - 1][j][1]))
                if h < n_heads:
                    pieces[h].append(piece(h, j, scores[h][j]))

        outs = []
        for h in range(n_heads):
            m_all = functools.reduce(jnp.maximum, [m for m, _ in pieces[h]])
            acc = sum(jnp.exp(m - m_all) * pv for (m, _), pv in zip(pieces[h], pvs[h]))
            outs.append(acc[:HEAD_DIM, :] / acc[HEAD_DIM:HEAD_DIM + 1, :])
        o_ref[...] = jnp.concatenate(outs, axis=0).T.astype(o_ref.dtype)

    for ib in range(n_blocks):
        pl.when(i == ib)(functools.partial(query_block, ib))


def _dilated_bias(n_blocks):
    tb = ATT_BLOCK
    shape = (n_blocks, tb, tb)
    d = n_blocks - 1 - lax.broadcasted_iota(jnp.int32, shape, 0)
    c = lax.broadcasted_iota(jnp.int32, shape, 1)
    r = lax.broadcasted_iota(jnp.int32, shape, 2)
    delta = d * tb + r - c
    mult = jnp.zeros(shape, F32)
    for window, dilation in A_PATTERNS:
        hit = (delta >= 0) & (delta <= window) & (delta % dilation == 0)
        mult = mult + hit.astype(F32)
    return jnp.log(mult).reshape(n_blocks * tb, tb)


def _attention(proj, q_col, mode, w_stack, layer, name):
    b, s, _ = proj.shape
    tb = ATT_BLOCK
    n_blocks = s // tb
    gw = ATT_HEADS_PER_STEP * HEAD_DIM
    n_groups = ATT_WIDTH // gw
    q_blk = q_col * LANES // gw
    k_blk, v_blk = q_blk + n_groups, q_blk + 2 * n_groups
    w_rows, w_cols = w_stack.shape[1:]
    n_slabs = b * n_groups * n_blocks // 2
    slab = w_rows // n_slabs
    assert slab * n_slabs == w_rows and slab % BF16_SUBLANES == 0
    slab_index = lambda bi, g, i: ((bi * n_groups + g) * n_blocks + i) // 2
    in_specs = [
        pl.BlockSpec((None, tb, gw), lambda bi, g, i: (bi, i, q_blk + g)),
        pl.BlockSpec((None, s, gw), lambda bi, g, i: (bi, 0, k_blk + g)),
        pl.BlockSpec((None, s, gw), lambda bi, g, i: (bi, 0, v_blk + g)),
        pl.BlockSpec((None, slab, w_cols), lambda bi, g, i: (layer, slab_index(bi, g, i), 0)),
    ]
    args = [proj, proj, proj, w_stack]
    scratch = [pltpu.VMEM((gw, s), BF16)]
    if mode == "dilated":
        in_specs += [_resident_spec((s, tb), lambda bi, g, i: (0, 0))]
        args += [_dilated_bias(n_blocks)]
    else:
        scratch += [pltpu.VMEM((ONES_ROWS, gw), F32)]
    out, w_bf16 = pl.pallas_call(
        functools.partial(_attn_kernel, mode=mode, n_blocks=n_blocks),
        grid=(b, n_groups, n_blocks),
        in_specs=in_specs,
        out_specs=[pl.BlockSpec((None, tb, gw), lambda bi, g, i: (bi, i, g)),
                   pl.BlockSpec((slab, w_cols), lambda bi, g, i: (slab_index(bi, g, i), 0))],
        out_shape=[jax.ShapeDtypeStruct((b, s, ATT_WIDTH), BF16),
                   jax.ShapeDtypeStruct((w_rows, w_cols), BF16)],
        scratch_shapes=scratch,
        compiler_params=_params("arbitrary", "arbitrary", "arbitrary"),
        name=name,
    )(*args)
    return out.reshape(b * s, ATT_WIDTH), w_bf16


def _spatial_gate(u, v, g_ref, b_ref, w_ref, bias_ref):
    u = _gelu_tanh(u)
    v = _layer_norm(_gelu_tanh(v), g_ref[...], b_ref[...])
    lane = lax.broadcasted_iota(jnp.int32, (C_CHUNK, LANES), 1)
    t_idx = lax.broadcasted_iota(jnp.int32, (C_CHUNK, C_CHUNK), 0)
    s_idx = lax.broadcasted_iota(jnp.int32, (C_CHUNK, C_CHUNK), 1)
    w = [jnp.where(s_idx <= t_idx, w_ref[g], 0.0).astype(BF16) for g in range(C_GROUPS)]
    per_tile = LANES // C_GROUP_DIM
    out_rows = []
    for ch in range(u.shape[0] // C_CHUNK):
        rows = slice(ch * C_CHUNK, (ch + 1) * C_CHUNK)
        out_cols = []
        for p in range(C_WIDTH // LANES):
            cols = slice(p * LANES, (p + 1) * LANES)
            v2 = v[rows, cols]
            sv = bias_ref[:, cols]
            for q in range(per_tile):
                in_group = (lane >= q * C_GROUP_DIM) & (lane < (q + 1) * C_GROUP_DIM)
                vq = jnp.where(in_group, v2, 0.0).astype(BF16)
                sv = sv + jnp.where(in_group, _dot(w[p * per_tile + q], vq), 0.0)
            out_cols.append((u[rows, cols] * sv).astype(BF16))
        out_rows.append(jnp.concatenate(out_cols, axis=1))
    return jnp.concatenate(out_rows, axis=0)


def _emit_norm(z, g_ref, b_ref, out_refs):
    y = _layer_norm(z, g_ref[...], b_ref[...])
    for o_ref in out_refs:
        o_ref[...] = y.astype(o_ref.dtype)


def _merge_kernel(oa_ref, ob_ref, u_ref, v_ref, ga_ref, gb_ref, gc_ref, x_ref,
                  cg_ref, cb_ref, cw_ref, cbias_ref,
                  wa_ref, wb_ref, wc_ref, wo_ref, g_ref, b_ref, *out_refs, alpha):
    def branch(y, w_ref, gate_ref):
        gate = 0.5 * jnp.tanh(0.5 * gate_ref[...].astype(F32)) + 0.5
        return gate * _dot(y, w_ref[...])

    yc = _spatial_gate(u_ref[...].astype(F32), v_ref[...].astype(F32), cg_ref, cb_ref, cw_ref, cbias_ref)
    merged = (branch(oa_ref[...], wa_ref, ga_ref) + branch(ob_ref[...], wb_ref, gb_ref)
              + branch(yc, wc_ref, gc_ref))
    mix = _dot(merged.astype(BF16), wo_ref[...])
    _emit_norm(alpha * x_ref[...] + mix, g_ref, b_ref, out_refs)


def _merge(oa, ob, proj2d, u_col, v_col, gate_col, x, c_ln_g, c_ln_b, c_ws, c_bs,
           wa, wb, wc, wo, ln_g, ln_b, alpha, tm):
    t, d = x.shape
    row = lambda i: (i, 0)
    const = lambda i: (0, 0)
    c_bias = jnp.repeat(c_bs.T, C_GROUP_DIM, axis=1)
    branch_spec = pl.BlockSpec((tm, ATT_WIDTH), row)
    uv_specs = [pl.BlockSpec((tm, C_WIDTH), lambda i, c=c: (i, c)) for c in (u_col, v_col)]
    gate_specs = [pl.BlockSpec((tm, d), lambda i, c=gate_col + n: (i, c)) for n in range(3)]
    c_specs = [pl.BlockSpec((1, C_WIDTH), const), pl.BlockSpec((1, C_WIDTH), const),
               pl.BlockSpec((C_GROUPS, C_CHUNK, C_CHUNK), lambda i: (0, 0, 0)),
               pl.BlockSpec((C_CHUNK, C_WIDTH), const)]
    w_br_spec = _resident_spec((ATT_WIDTH, d), const)
    vec_spec = pl.BlockSpec((1, d), const)
    return pl.pallas_call(
        functools.partial(_merge_kernel, alpha=alpha),
        grid=(t // tm,),
        in_specs=[branch_spec, branch_spec, *uv_specs, *gate_specs, pl.BlockSpec((tm, d), row), *c_specs,
                  w_br_spec, w_br_spec, w_br_spec, _resident_spec((d, d), const), vec_spec, vec_spec],
        out_specs=[pl.BlockSpec((tm, d), row), pl.BlockSpec((tm, d), row)],
        out_shape=[jax.ShapeDtypeStruct((t, d), F32), jax.ShapeDtypeStruct((t, d), BF16)],
        compiler_params=_params("parallel"),
        name="merge",
    )(oa, ob, proj2d, proj2d, proj2d, proj2d, proj2d, x,
      c_ln_g.reshape(1, -1), c_ln_b.reshape(1, -1), c_ws, c_bias,
      wa, wb, wc, wo, ln_g.reshape(1, -1), ln_b.reshape(1, -1))


def _cross_kernel(x_ref, xb_ref, k_ref, v_ref, wq_ref, wo_ref, g_ref, b_ref, *out_refs, alpha):
    q = (_dot(xb_ref[...], wq_ref[...]) * (X_HEAD_DIM ** -0.5)).astype(BF16)
    heads = []
    for h in range(X_HEADS):
        cols = slice(h * X_HEAD_DIM, (h + 1) * X_HEAD_DIM)
        s = _dot_nt(q[:, cols], k_ref[:, cols])
        p = jnp.exp(s - jnp.max(s, axis=-1, keepdims=True))
        o = _dot(p.astype(BF16), v_ref[:, cols]) / jnp.sum(p, axis=-1, keepdims=True)
        heads.append(o.astype(BF16))
    cross = _dot(jnp.concatenate(heads, axis=-1), wo_ref[...])
    _emit_norm(alpha * x_ref[...] + cross, g_ref, b_ref, out_refs)


def _cross(x, xb, kv, wq, wo, ln_g, ln_b, alpha, batch, tm):
    t, d = x.shape
    s = t // batch
    m = kv.shape[1]
    x3, xb3 = x.reshape(batch, s, d), xb.reshape(batch, s, d)
    row = lambda bi, i: (bi, i, 0)
    const = lambda bi, i: (0, 0)
    outs = pl.pallas_call(
        functools.partial(_cross_kernel, alpha=alpha),
        grid=(batch, s // tm),
        in_specs=[pl.BlockSpec((None, tm, d), row), pl.BlockSpec((None, tm, d), row),
                  pl.BlockSpec((None, m, X_WIDTH), lambda bi, i: (bi, 0, 0)),
                  pl.BlockSpec((None, m, X_WIDTH), lambda bi, i: (bi, 0, 1)),
                  pl.BlockSpec((d, X_WIDTH), const), pl.BlockSpec((X_WIDTH, d), const),
                  pl.BlockSpec((1, d), const), pl.BlockSpec((1, d), const)],
        out_specs=[pl.BlockSpec((None, tm, d), row), pl.BlockSpec((None, tm, d), row)],
        out_shape=[jax.ShapeDtypeStruct((batch, s, d), F32), jax.ShapeDtypeStruct((batch, s, d), BF16)],
        compiler_params=_params("parallel", "parallel"),
        name="cross",
    )(x3, xb3, kv, kv, wq, wo, ln_g.reshape(1, -1), ln_b.reshape(1, -1))
    return outs[0].reshape(t, d), outs[1].reshape(t, d)


def _ffn_kernel(x_ref, xb_ref, w1_ref, w2_ref, g_ref, b_ref, *out_refs, alpha):
    acc_ref = out_refs[0]
    j = pl.program_id(1)

    @pl.when(j == 0)
    def _():
        acc_ref[...] = jnp.zeros_like(acc_ref)

    h = jnp.maximum(_dot(xb_ref[...], w1_ref[...]), 0.0)
    acc_ref[...] += _dot((h * h).astype(BF16), w2_ref[...])

    @pl.when(j == pl.num_programs(1) - 1)
    def _():
        _emit_norm(alpha * x_ref[...] + acc_ref[...], g_ref, b_ref, out_refs)


def _ffn(x, xb, w1, w2, ln_g, ln_b, alpha, tm, tf, with_bf16):
    t, d = x.shape
    d_ff = w1.shape[1]
    row = lambda i, j: (i, 0)
    const = lambda i, j: (0, 0)
    out_dtypes = [F32, BF16] if with_bf16 else [F32]
    return pl.pallas_call(
        functools.partial(_ffn_kernel, alpha=alpha),
        grid=(t // tm, d_ff // tf),
        in_specs=[pl.BlockSpec((tm, d), row), pl.BlockSpec((tm, d), row),
                  pl.BlockSpec((d, tf), lambda i, j: (0, j)), pl.BlockSpec((tf, d), lambda i, j: (j, 0)),
                  pl.BlockSpec((1, d), const), pl.BlockSpec((1, d), const)],
        out_specs=[pl.BlockSpec((tm, d), row) for _ in out_dtypes],
        out_shape=[jax.ShapeDtypeStruct((t, d), dt) for dt in out_dtypes],
        compiler_params=_params("parallel", "arbitrary"),
        name="ffn",
    )(x, xb, w1, w2, ln_g.reshape(1, -1), ln_b.reshape(1, -1))


def kernel(x, mem, w_in, w_br_a, w_br_b, w_br_c, w_out, c_ln_g, c_ln_b, c_ws, c_bs,
           ln1_g, ln1_b, w_xq, w_xk, w_xv, w_xo, ln2_g, ln2_b, w_ff1, w_ff2, ln3_g, ln3_b):
    batch, seq, d = x.shape
    depth = w_in.shape[0]
    t = batch * seq
    alpha = (2 * depth) ** 0.25
    assert all(seq % window == 0 for window, _ in A_PATTERNS) and seq % ATT_BLOCK == 0
    assert w_in.shape[2] == 3 * ATT_WIDTH * 2 + 2 * C_WIDTH + 3 * d

    qa_blk, qb_blk = 0, 3 * ATT_WIDTH // LANES
    u_blk, v_blk = 6, 7
    gate_blk = (6 * ATT_WIDTH + 2 * C_WIDTH) // d

    xf = x.reshape(t, d)
    xb = xf
    mem_b = mem.reshape(-1, d).astype(BF16)
    for l in range(depth):
        proj2d = _matmul_f32w(xb, w_in, l, BF16, 1024, 1024, "in_proj")
        proj = proj2d.reshape(batch, seq, -1)
        oa, w1 = _attention(proj, qa_blk, "dilated", w_ff1, l, "dilated_attn")
        ob, w2 = _attention(proj, qb_blk, "moba", w_ff2, l, "moba_attn")
        xf, xb = _merge(oa, ob, proj2d, u_blk, v_blk, gate_blk, xf,
                        c_ln_g[l], c_ln_b[l], c_ws[l], c_bs[l],
                        w_br_a[l].astype(BF16), w_br_b[l].astype(BF16), w_br_c[l].astype(BF16),
                        w_out[l].astype(BF16), ln1_g[l], ln1_b[l], alpha, 256)
        w_kv = jnp.concatenate([w_xk[l], w_xv[l]], axis=1).astype(BF16)
        kv = _matmul(mem_b, w_kv, BF16, 1024, 1024, "mem_proj").reshape(batch, -1, 2 * X_WIDTH)
        xf, xb = _cross(xf, xb, kv, w_xq[l].astype(BF16), w_xo[l].astype(BF16),
                        ln2_g[l], ln2_b[l], alpha, batch, 512)
        last = l == depth - 1
        outs = _ffn(xf, xb, w1, w2,
                    ln3_g[l], ln3_b[l], alpha, 512, 1024, not last)
        xf = outs[0]
        xb = None if last else outs[1]
    return xf.reshape(batch, seq, d)
```

```python
import functools

import jax
import jax.numpy as jnp
from jax import lax
from jax.experimental import pallas as pl
from jax.experimental.pallas import tpu as pltpu

F32 = jnp.float32
BF16 = jnp.bfloat16

HEAD_DIM = 64
A_PATTERNS = ((128, 1), (512, 4), (2048, 16))
ATT_WIDTH = 512
MOBA_BLOCK = 256
MOBA_TOPK = 3
C_GROUPS = 8
C_GROUP_DIM = 64
C_WIDTH = C_GROUPS * C_GROUP_DIM
C_CHUNK = 128
X_HEADS = 4
X_HEAD_DIM = 128
X_WIDTH = X_HEADS * X_HEAD_DIM
LN_EPS = 1e-5
LOG2_E = 1.4426950408889634

LANES = 128
ATT_BLOCK = 256
ATT_HEADS_PER_STEP = 8
ATT_QUERY_BLOCKS_PER_STEP = 2
BF16_SUBLANES = 16
ONES_ROWS = BF16_SUBLANES
VMEM_LIMIT = 56 * 1024 * 1024


def _params(*sem):
    return pltpu.CompilerParams(dimension_semantics=sem, vmem_limit_bytes=VMEM_LIMIT)


def _resident_spec(shape, index_map):
    return pl.BlockSpec(shape, index_map, pipeline_mode=pl.Buffered(1))


def _layer_norm(z, g, b):
    mu = jnp.mean(z, axis=-1, keepdims=True)
    zc = z - mu
    var = jnp.mean(zc * zc, axis=-1, keepdims=True)
    return zc * lax.rsqrt(var + LN_EPS) * g + b


def _gelu_tanh(x):
    c = (2.0 / jnp.pi) ** 0.5
    return 0.5 * x * (1.0 + jnp.tanh(c * (x + 0.044715 * (x * x * x))))


def _dot(a, b):
    return jnp.dot(a, b, preferred_element_type=F32)


def _dot_nt(a, b):
    return lax.dot_general(a, b, (((1,), (1,)), ((), ())), preferred_element_type=F32)


def _matmul_kernel(a_ref, w_ref, o_ref):
    o_ref[...] = _dot(a_ref[...], w_ref[...]).astype(o_ref.dtype)


def _matmul(a, w, out_dtype, tm, tn, name):
    m, k = a.shape
    n = w.shape[1]
    tm, tn = min(tm, m), min(tn, n)
    assert m % tm == 0 and n % tn == 0
    return pl.pallas_call(
        _matmul_kernel,
        grid=(m // tm, n // tn),
        in_specs=[pl.BlockSpec((tm, k), lambda i, j: (i, 0)),
                  pl.BlockSpec((k, tn), lambda i, j: (0, j))],
        out_specs=pl.BlockSpec((tm, tn), lambda i, j: (i, j)),
        out_shape=jax.ShapeDtypeStruct((m, n), out_dtype),
        compiler_params=_params("parallel", "parallel"),
        name=name,
    )(a, w)


def _matmul_f32w_kernel(a_ref, w_ref, o_ref, wb_ref):
    @pl.when(pl.program_id(1) == 0)
    def _():
        wb_ref[...] = w_ref[...].astype(BF16)

    o_ref[...] = _dot(a_ref[...].astype(BF16), wb_ref[...]).astype(o_ref.dtype)


def _matmul_f32w(a, w_stack, layer, out_dtype, tm, tn, name):
    m, k = a.shape
    n = w_stack.shape[2]
    assert m % tm == 0 and n % tn == 0
    return pl.pallas_call(
        _matmul_f32w_kernel,
        grid=(n // tn, m // tm),
        in_specs=[pl.BlockSpec((tm, k), lambda j, i: (i, 0)),
                  pl.BlockSpec((None, k, tn), lambda j, i: (layer, 0, j))],
        out_specs=pl.BlockSpec((tm, tn), lambda j, i: (i, j)),
        out_shape=jax.ShapeDtypeStruct((m, n), out_dtype),
        scratch_shapes=[pltpu.VMEM((k, tn), BF16)],
        compiler_params=_params("parallel", "arbitrary"),
        name=name,
    )(a, w_stack)


def _moba_selection(kmean, q2, head, n_past):
    klane = lax.broadcasted_iota(jnp.int32, kmean.shape, 1)
    km = jnp.where((klane < HEAD_DIM) == (head == 0), kmean, 0.0)
    km_hi = km.astype(BF16)
    km_lo = (km - km_hi.astype(F32)).astype(BF16)
    gate = _dot_nt(km_hi, q2) + _dot_nt(km_lo, q2)
    row = lax.broadcasted_iota(jnp.int32, gate.shape, 0)
    gate = jnp.where(row < n_past, gate, -jnp.inf)
    sel = []
    for n in range(n_past):
        gn = gate[n:n + 1, :]
        ahead = jnp.where(gate > gn, 1.0, 0.0) + jnp.where((gate == gn) & (row < n), 1.0, 0.0)
        sel.append(jnp.where(jnp.sum(ahead, axis=0, keepdims=True) < MOBA_TOPK, 1.0, 0.0))
    return sel


def _attn_kernel(*refs, mode, n_blocks):
    if mode == "dilated":
        q_ref, k_ref, v_ref, w_ref, bias_ref, o_ref, wb_ref, vt_ref = refs
    else:
        q_ref, k_ref, v_ref, w_ref, o_ref, wb_ref, vt_ref, kmean_ref = refs
    wb_ref[...] = w_ref[...].astype(BF16)
    tb = ATT_BLOCK
    n_slots = q_ref.shape[0]
    n_steps = n_blocks // n_slots
    n_heads = q_ref.shape[2] // HEAD_DIM
    i = pl.program_id(2)
    lane = lax.broadcasted_iota(jnp.int32, (tb, LANES), 1)
    tile = lambda h: slice(h // 2 * LANES, (h // 2 + 1) * LANES)
    q2, qm = [], []
    for slot in range(n_slots):
        q2.append([])
        qm.append([])
        for t in range(n_heads // 2):
            qt = q_ref[slot, :, t * LANES:(t + 1) * LANES]
            q2[slot].append(qt * jnp.asarray(HEAD_DIM ** -0.5, BF16))
            ql = (qt.astype(F32) * (HEAD_DIM ** -0.5 * LOG2_E)).astype(BF16)
            zero = jnp.zeros_like(ql)
            qm[slot] += [jnp.where(lane < HEAD_DIM, ql, zero), jnp.where(lane >= HEAD_DIM, ql, zero)]

    @pl.when(i == 0)
    def _():
        for t in range(n_heads // 2):
            cols = slice(t * LANES, (t + 1) * LANES)
            for n in range(n_blocks):
                rows = slice(n * tb, (n + 1) * tb)
                vt_ref[cols, rows] = v_ref[rows, cols].astype(F32).T.astype(BF16)
        if mode == "moba":
            kmean_ref[...] = jnp.zeros_like(kmean_ref)
            for n in range(n_blocks):
                kb = k_ref[n * tb:(n + 1) * tb, :].astype(F32)
                kmean_ref[n:n + 1, :] = jnp.sum(kb, axis=0, keepdims=True) / tb

    if mode == "moba":
        key_pos = lax.broadcasted_iota(jnp.int32, (tb, tb), 0)
        qry_pos = lax.broadcasted_iota(jnp.int32, (tb, tb), 1)

    ones_rows = jnp.ones((ONES_ROWS, tb), BF16)
    key_rows = lambda j: slice(j * tb, (j + 1) * tb)

    def step(step_index):
        items = [(slot, step_index + slot * n_steps, h) for slot in range(n_slots) for h in range(n_heads)]
        sel = {}
        if mode == "moba":
            for slot, ib, h in items:
                if ib > MOBA_TOPK:
                    sel[slot, h] = _moba_selection(kmean_ref[:, tile(h)], q2[slot][h // 2], h % 2, ib)

        def score(item, j):
            slot, _, h = item
            return _dot_nt(k_ref[key_rows(j), tile(h)], qm[slot][h])

        def piece(item, j, s):
            slot, ib, h = item
            if mode == "dilated":
                off = (n_blocks - 1 - ib + j) * tb
                s = s + bias_ref[off:off + tb, :]
                m = jnp.max(s, axis=0, keepdims=True)
                return m, jnp.exp2(s - jnp.where(m == -jnp.inf, 0.0, m)).astype(BF16)
            if j == ib:
                s = jnp.where(key_pos <= qry_pos, s, -jnp.inf)
            m = jnp.max(s, axis=0, keepdims=True)
            p = jnp.exp2(s - m).astype(BF16)
            if j < ib and (slot, h) in sel:
                m = jnp.where(sel[slot, h][j] > 0.0, m, -jnp.inf)
            return m, p

        def weighted(item, j, p):
            h = item[2]
            vext = jnp.concatenate([vt_ref[h * HEAD_DIM:(h + 1) * HEAD_DIM, key_rows(j)], ones_rows], axis=0)
            return _dot(vext, p)

        n_keys = [ib + 1 for _, ib, _ in items]
        n_items = len(items)
        scores = [[score(items[0], j) for j in range(n_keys[0])]] + [[] for _ in range(n_items - 1)]
        pieces = [[] for _ in range(n_items)]
        pvs = [[] for _ in range(n_items)]
        for k in range(n_items + 1):
            for j in range(max(n_keys[max(k - 1, 0):k + 2])):
                if k + 1 < n_items and j < n_keys[k + 1]:
                    scores[k + 1].append(score(items[k + 1], j))
                if k >= 1 and j < n_keys[k - 1]:
                    pvs[k - 1].append(weighted(items[k - 1], j, pieces[k - 1][j][1]))
                if k < n_items and j < n_keys[k]:
                    pieces[k].append(piece(items[k], j, scores[k][j]))

        for slot in range(n_slots):
            outs = []
            for k in range(slot * n_heads, (slot + 1) * n_heads):
                m_all = functools.reduce(jnp.maximum, [m for m, _ in pieces[k]])
                acc = sum(jnp.exp2(m - m_all) * pv for (m, _), pv in zip(pieces[k], pvs[k]))
                outs.append(acc[:HEAD_DIM, :] / acc[HEAD_DIM:HEAD_DIM + 1, :])
            o_ref[slot] = jnp.concatenate(outs, axis=0).T.astype(o_ref.dtype)

    for step_index in range(n_steps):
        pl.when(i == step_index)(functools.partial(step, step_index))


def _dilated_bias(n_blocks):
    tb = ATT_BLOCK
    shape = (n_blocks, tb, tb)
    d = n_blocks - 1 - lax.broadcasted_iota(jnp.int32, shape, 0)
    c = lax.broadcasted_iota(jnp.int32, shape, 1)
    r = lax.broadcasted_iota(jnp.int32, shape, 2)
    delta = d * tb + r - c
    mult = jnp.zeros(shape, F32)
    for window, dilation in A_PATTERNS:
        hit = (delta >= 0) & (delta <= window) & (delta % dilation == 0)
        mult = mult + hit.astype(F32)
    return jnp.log2(mult).reshape(n_blocks * tb, tb)


def _attention(proj, q_col, mode, w_stack, layer, name):
    b, s, _ = proj.shape
    tb = ATT_BLOCK
    n_blocks = s // tb
    gw = ATT_HEADS_PER_STEP * HEAD_DIM
    n_groups = ATT_WIDTH // gw
    q_blk = q_col * LANES // gw
    k_blk, v_blk = q_blk + n_groups, q_blk + 2 * n_groups
    n_slots = ATT_QUERY_BLOCKS_PER_STEP
    n_steps = n_blocks // n_slots
    assert n_steps * n_slots == n_blocks
    proj_q = proj.reshape(b, n_slots, n_steps, tb, proj.shape[2])
    w_rows, w_cols = w_stack.shape[1:]
    n_slabs = b * n_groups * n_steps
    slab = w_rows // n_slabs
    assert slab * n_slabs == w_rows and slab % BF16_SUBLANES == 0
    slab_index = lambda bi, g, i: (bi * n_groups + g) * n_steps + i
    in_specs = [
        pl.BlockSpec((None, n_slots, None, tb, gw), lambda bi, g, i: (bi, 0, i, 0, q_blk + g)),
        pl.BlockSpec((None, s, gw), lambda bi, g, i: (bi, 0, k_blk + g)),
        pl.BlockSpec((None, s, gw), lambda bi, g, i: (bi, 0, v_blk + g)),
        pl.BlockSpec((None, slab, w_cols), lambda bi, g, i: (layer, slab_index(bi, g, i), 0)),
    ]
    args = [proj_q, proj, proj, w_stack]
    scratch = [pltpu.VMEM((gw, s), BF16)]
    if mode == "dilated":
        in_specs += [_resident_spec((s, tb), lambda bi, g, i: (0, 0))]
        args += [_dilated_bias(n_blocks)]
    else:
        scratch += [pltpu.VMEM((ONES_ROWS, gw), F32)]
    out, w_bf16 = pl.pallas_call(
        functools.partial(_attn_kernel, mode=mode, n_blocks=n_blocks),
        grid=(b, n_groups, n_steps),
        in_specs=in_specs,
        out_specs=[pl.BlockSpec((None, n_slots, None, tb, gw), lambda bi, g, i: (bi, 0, i, 0, g)),
                   pl.BlockSpec((slab, w_cols), lambda bi, g, i: (slab_index(bi, g, i), 0))],
        out_shape=[jax.ShapeDtypeStruct((b, n_slots, n_steps, tb, ATT_WIDTH), BF16),
                   jax.ShapeDtypeStruct((w_rows, w_cols), BF16)],
        scratch_shapes=scratch,
        compiler_params=_params("arbitrary", "arbitrary", "arbitrary"),
        name=name,
    )(*args)
    return out.reshape(b * s, ATT_WIDTH), w_bf16


def _spatial_gate(u, v, g_ref, b_ref, w_ref, bias_ref):
    u = _gelu_tanh(u)
    v = _layer_norm(_gelu_tanh(v), g_ref[...], b_ref[...])
    lane = lax.broadcasted_iota(jnp.int32, (C_CHUNK, LANES), 1)
    t_idx = lax.broadcasted_iota(jnp.int32, (C_CHUNK, C_CHUNK), 0)
    s_idx = lax.broadcasted_iota(jnp.int32, (C_CHUNK, C_CHUNK), 1)
    w = [jnp.where(s_idx <= t_idx, w_ref[g], 0.0).astype(BF16) for g in range(C_GROUPS)]
    per_tile = LANES // C_GROUP_DIM
    out_rows = []
    for ch in range(u.shape[0] // C_CHUNK):
        rows = slice(ch * C_CHUNK, (ch + 1) * C_CHUNK)
        out_cols = []
        for p in range(C_WIDTH // LANES):
            cols = slice(p * LANES, (p + 1) * LANES)
            v2 = v[rows, cols]
            sv = bias_ref[:, cols]
            for q in range(per_tile):
                in_group = (lane >= q * C_GROUP_DIM) & (lane < (q + 1) * C_GROUP_DIM)
                vq = jnp.where(in_group, v2, 0.0).astype(BF16)
                sv = sv + jnp.where(in_group, _dot(w[p * per_tile + q], vq), 0.0)
            out_cols.append((u[rows, cols] * sv).astype(BF16))
        out_rows.append(jnp.concatenate(out_cols, axis=1))
    return jnp.concatenate(out_rows, axis=0)


def _emit_norm(z, g_ref, b_ref, out_refs):
    y = _layer_norm(z, g_ref[...], b_ref[...])
    for o_ref in out_refs:
        o_ref[...] = y.astype(o_ref.dtype)


def _merge_kernel(oa_ref, ob_ref, u_ref, v_ref, ga_ref, gb_ref, gc_ref, x_ref,
                  cg_ref, cb_ref, cw_ref, cbias_ref,
                  wa_ref, wb_ref, wc_ref, wo_ref, g_ref, b_ref, *out_refs, alpha):
    def branch(y, w_ref, gate_ref):
        gate = 0.5 * jnp.tanh(0.5 * gate_ref[...].astype(F32)) + 0.5
        return gate * _dot(y, w_ref[...])

    yc = _spatial_gate(u_ref[...].astype(F32), v_ref[...].astype(F32), cg_ref, cb_ref, cw_ref, cbias_ref)
    merged = (branch(oa_ref[...], wa_ref, ga_ref) + branch(ob_ref[...], wb_ref, gb_ref)
              + branch(yc, wc_ref, gc_ref))
    mix = _dot(merged.astype(BF16), wo_ref[...])
    _emit_norm(alpha * x_ref[...] + mix, g_ref, b_ref, out_refs)


def _merge(oa, ob, proj2d, u_col, v_col, gate_col, x, c_ln_g, c_ln_b, c_ws, c_bs,
           wa, wb, wc, wo, ln_g, ln_b, alpha, tm):
    t, d = x.shape
    row = lambda i: (i, 0)
    const = lambda i: (0, 0)
    c_bias = jnp.repeat(c_bs.T, C_GROUP_DIM, axis=1)
    branch_spec = pl.BlockSpec((tm, ATT_WIDTH), row)
    uv_specs = [pl.BlockSpec((tm, C_WIDTH), lambda i, c=c: (i, c)) for c in (u_col, v_col)]
    gate_specs = [pl.BlockSpec((tm, d), lambda i, c=gate_col + n: (i, c)) for n in range(3)]
    c_specs = [pl.BlockSpec((1, C_WIDTH), const), pl.BlockSpec((1, C_WIDTH), const),
               pl.BlockSpec((C_GROUPS, C_CHUNK, C_CHUNK), lambda i: (0, 0, 0)),
               pl.BlockSpec((C_CHUNK, C_WIDTH), const)]
    w_br_spec = _resident_spec((ATT_WIDTH, d), const)
    vec_spec = pl.BlockSpec((1, d), const)
    return pl.pallas_call(
        functools.partial(_merge_kernel, alpha=alpha),
        grid=(t // tm,),
        in_specs=[branch_spec, branch_spec, *uv_specs, *gate_specs, pl.BlockSpec((tm, d), row), *c_specs,
                  w_br_spec, w_br_spec, w_br_spec, _resident_spec((d, d), const), vec_spec, vec_spec],
        out_specs=[pl.BlockSpec((tm, d), row), pl.BlockSpec((tm, d), row)],
        out_shape=[jax.ShapeDtypeStruct((t, d), F32), jax.ShapeDtypeStruct((t, d), BF16)],
        compiler_params=_params("parallel"),
        name="merge",
    )(oa, ob, proj2d, proj2d, proj2d, proj2d, proj2d, x,
      c_ln_g.reshape(1, -1), c_ln_b.reshape(1, -1), c_ws, c_bias,
      wa, wb, wc, wo, ln_g.reshape(1, -1), ln_b.reshape(1, -1))


def _cross_kernel(x_ref, xb_ref, k_ref, v_ref, wq_ref, wo_ref, g_ref, b_ref, *out_refs, alpha):
    q = (_dot(xb_ref[...], wq_ref[...]) * (X_HEAD_DIM ** -0.5)).astype(BF16)
    cols = [slice(h * X_HEAD_DIM, (h + 1) * X_HEAD_DIM) for h in range(X_HEADS)]
    scores = [_dot_nt(q[:, c], k_ref[:, c]) for c in cols]
    probs = [jnp.exp(s - jnp.max(s, axis=-1, keepdims=True)) for s in scores]
    heads = [(_dot(p.astype(BF16), v_ref[:, c]) / jnp.sum(p, axis=-1, keepdims=True)).astype(BF16)
             for p, c in zip(probs, cols)]
    cross = _dot(jnp.concatenate(heads, axis=-1), wo_ref[...])
    _emit_norm(alpha * x_ref[...] + cross, g_ref, b_ref, out_refs)


def _cross(x, xb, kv, wq, wo, ln_g, ln_b, alpha, batch, tm):
    t, d = x.shape
    s = t // batch
    m = kv.shape[1]
    x3, xb3 = x.reshape(batch, s, d), xb.reshape(batch, s, d)
    row = lambda bi, i: (bi, i, 0)
    const = lambda bi, i: (0, 0)
    outs = pl.pallas_call(
        functools.partial(_cross_kernel, alpha=alpha),
        grid=(batch, s // tm),
        in_specs=[pl.BlockSpec((None, tm, d), row), pl.BlockSpec((None, tm, d), row),
                  pl.BlockSpec((None, m, X_WIDTH), lambda bi, i: (bi, 0, 0)),
                  pl.BlockSpec((None, m, X_WIDTH), lambda bi, i: (bi, 0, 1)),
                  pl.BlockSpec((d, X_WIDTH), const), pl.BlockSpec((X_WIDTH, d), const),
                  pl.BlockSpec((1, d), const), pl.BlockSpec((1, d), const)],
        out_specs=[pl.BlockSpec((None, tm, d), row), pl.BlockSpec((None, tm, d), row)],
        out_shape=[jax.ShapeDtypeStruct((batch, s, d), F32), jax.ShapeDtypeStruct((batch, s, d), BF16)],
        compiler_params=_params("parallel", "parallel"),
        name="cross",
    )(x3, xb3, kv, kv, wq, wo, ln_g.reshape(1, -1), ln_b.reshape(1, -1))
    return outs[0].reshape(t, d), outs[1].reshape(t, d)


def _ffn_kernel(x_ref, xb_ref, w1_ref, w2_ref, g_ref, b_ref, *out_refs, alpha):
    acc_ref = out_refs[0]
    j = pl.program_id(1)

    @pl.when(j == 0)
    def _():
        acc_ref[...] = jnp.zeros_like(acc_ref)

    h = jnp.maximum(_dot(xb_ref[...], w1_ref[...]), 0.0)
    acc_ref[...] += _dot((h * h).astype(BF16), w2_ref[...])

    @pl.when(j == pl.num_programs(1) - 1)
    def _():
        _emit_norm(alpha * x_ref[...] + acc_ref[...], g_ref, b_ref, out_refs)


def _ffn(x, xb, w1, w2, ln_g, ln_b, alpha, tm, tf, with_bf16):
    t, d = x.shape
    d_ff = w1.shape[1]
    row = lambda i, j: (i, 0)
    const = lambda i, j: (0, 0)
    out_dtypes = [F32, BF16] if with_bf16 else [F32]
    return pl.pallas_call(
        functools.partial(_ffn_kernel, alpha=alpha),
        grid=(t // tm, d_ff // tf),
        in_specs=[pl.BlockSpec((tm, d), row), pl.BlockSpec((tm, d), row),
                  pl.BlockSpec((d, tf), lambda i, j: (0, j)), pl.BlockSpec((tf, d), lambda i, j: (j, 0)),
                  pl.BlockSpec((1, d), const), pl.BlockSpec((1, d), const)],
        out_specs=[pl.BlockSpec((tm, d), row) for _ in out_dtypes],
        out_shape=[jax.ShapeDtypeStruct((t, d), dt) for dt in out_dtypes],
        compiler_params=_params("parallel", "arbitrary"),
        name="ffn",
    )(x, xb, w1, w2, ln_g.reshape(1, -1), ln_b.reshape(1, -1))


def kernel(x, mem, w_in, w_br_a, w_br_b, w_br_c, w_out, c_ln_g, c_ln_b, c_ws, c_bs,
           ln1_g, ln1_b, w_xq, w_xk, w_xv, w_xo, ln2_g, ln2_b, w_ff1, w_ff2, ln3_g, ln3_b):
    batch, seq, d = x.shape
    depth = w_in.shape[0]
    t = batch * seq
    alpha = (2 * depth) ** 0.25
    assert all(seq % window == 0 for window, _ in A_PATTERNS) and seq % ATT_BLOCK == 0
    assert w_in.shape[2] == 3 * ATT_WIDTH * 2 + 2 * C_WIDTH + 3 * d

    qa_blk, qb_blk = 0, 3 * ATT_WIDTH // LANES
    u_blk, v_blk = 6, 7
    gate_blk = (6 * ATT_WIDTH + 2 * C_WIDTH) // d

    xf = x.reshape(t, d)
    xb = xf
    mem_b = mem.reshape(-1, d).astype(BF16)
    for l in range(depth):
        proj2d = _matmul_f32w(xb, w_in, l, BF16, 1024, 1024, "in_proj")
        proj = proj2d.reshape(batch, seq, -1)
        oa, w1 = _attention(proj, qa_blk, "dilated", w_ff1, l, "dilated_attn")
        ob, w2 = _attention(proj, qb_blk, "moba", w_ff2, l, "moba_attn")
        xf, xb = _merge(oa, ob, proj2d, u_blk, v_blk, gate_blk, xf,
                        c_ln_g[l], c_ln_b[l], c_ws[l], c_bs[l],
                        w_br_a[l].astype(BF16), w_br_b[l].astype(BF16), w_br_c[l].astype(BF16),
                        w_out[l].astype(BF16), ln1_g[l], ln1_b[l], alpha, 256)
        w_kv = jnp.concatenate([w_xk[l], w_xv[l]], axis=1).astype(BF16)
        kv = _matmul(mem_b, w_kv, BF16, 1024, 1024, "mem_proj").reshape(batch, -1, 2 * X_WIDTH)
        xf, xb = _cross(xf, xb, kv, w_xq[l].astype(BF16), w_xo[l].astype(BF16),
                        ln2_g[l], ln2_b[l], alpha, batch, 512)
        last = l == depth - 1
        outs = _ffn(xf, xb, w1, w2,
                    ln3_g[l], ln3_b[l], alpha, 512, 1024, not last)
        xf = outs[0]
        xb = None if last else outs[1]
    return xf.reshape(batch, seq, d)
```

```python
import functools

import jax
import jax.numpy as jnp
from jax import lax
from jax.experimental import pallas as pl
from jax.experimental.pallas import tpu as pltpu

F32 = jnp.float32
BF16 = jnp.bfloat16

HEAD_DIM = 64
A_PATTERNS = ((128, 1), (512, 4), (2048, 16))
ATT_WIDTH = 512
MOBA_BLOCK = 256
MOBA_TOPK = 3
C_GROUPS = 8
C_GROUP_DIM = 64
C_WIDTH = C_GROUPS * C_GROUP_DIM
C_CHUNK = 128
X_HEADS = 4
X_HEAD_DIM = 128
X_WIDTH = X_HEADS * X_HEAD_DIM
LN_EPS = 1e-5
LOG2_E = 1.4426950408889634

LANES = 128
ATT_BLOCK = 256
ATT_HEADS_PER_STEP = 8
ATT_QUERY_BLOCKS_PER_STEP = 4
BF16_SUBLANES = 16
ONES_ROWS = BF16_SUBLANES
VMEM_LIMIT = 56 * 1024 * 1024

TILES = {"in_proj": (1024, 1024), "mem_proj": (1024, 1024), "merge": 256, "cross": 512, "ffn": (512, 1024)}


def _params(*sem):
    return pltpu.CompilerParams(dimension_semantics=sem, vmem_limit_bytes=VMEM_LIMIT)


def _resident_spec(shape, index_map):
    return pl.BlockSpec(shape, index_map, pipeline_mode=pl.Buffered(1))


def _layer_norm(z, g, b):
    mu = jnp.mean(z, axis=-1, keepdims=True)
    zc = z - mu
    var = jnp.mean(zc * zc, axis=-1, keepdims=True)
    return zc * lax.rsqrt(var + LN_EPS) * g + b


def _gelu_tanh(x):
    c = (2.0 / jnp.pi) ** 0.5
    return 0.5 * x * (1.0 + jnp.tanh(c * (x + 0.044715 * (x * x * x))))


def _dot(a, b):
    return jnp.dot(a, b, preferred_element_type=F32)


def _dot_nt(a, b):
    return lax.dot_general(a, b, (((1,), (1,)), ((), ())), preferred_element_type=F32)


def _matmul_kernel(a_ref, w_ref, o_ref):
    o_ref[...] = _dot(a_ref[...], w_ref[...]).astype(o_ref.dtype)


def _matmul(a, w, out_dtype, tm, tn, name):
    m, k = a.shape
    n = w.shape[1]
    tm, tn = min(tm, m), min(tn, n)
    assert m % tm == 0 and n % tn == 0
    return pl.pallas_call(
        _matmul_kernel,
        grid=(m // tm, n // tn),
        in_specs=[pl.BlockSpec((tm, k), lambda i, j: (i, 0)),
                  pl.BlockSpec((k, tn), lambda i, j: (0, j))],
        out_specs=pl.BlockSpec((tm, tn), lambda i, j: (i, j)),
        out_shape=jax.ShapeDtypeStruct((m, n), out_dtype),
        compiler_params=_params("parallel", "parallel"),
        name=name,
    )(a, w)


def _matmul_f32w_kernel(a_ref, w_ref, o_ref, wb_ref):
    def row_tile(first):
        if first:
            wb_ref[...] = w_ref[...].astype(BF16)
        o_ref[...] = _dot(a_ref[...].astype(BF16), wb_ref[...]).astype(o_ref.dtype)

    i = pl.program_id(1)
    pl.when(i == 0)(functools.partial(row_tile, True))
    pl.when(i > 0)(functools.partial(row_tile, False))


def _matmul_f32w(a, w_stack, layer, out_dtype, tm, tn, name):
    m, k = a.shape
    n = w_stack.shape[2]
    assert m % tm == 0 and n % tn == 0
    return pl.pallas_call(
        _matmul_f32w_kernel,
        grid=(n // tn, m // tm),
        in_specs=[pl.BlockSpec((tm, k), lambda j, i: (i, 0)),
                  pl.BlockSpec((None, k, tn), lambda j, i: (layer, 0, j))],
        out_specs=pl.BlockSpec((tm, tn), lambda j, i: (i, j)),
        out_shape=jax.ShapeDtypeStruct((m, n), out_dtype),
        scratch_shapes=[pltpu.VMEM((k, tn), BF16)],
        compiler_params=_params("parallel", "arbitrary"),
        name=name,
    )(a, w_stack)


def _moba_selection(kmean, q2, head, n_past):
    klane = lax.broadcasted_iota(jnp.int32, kmean.shape, 1)
    km = jnp.where((klane < HEAD_DIM) == (head == 0), kmean, 0.0)
    km_hi = km.astype(BF16)
    km_lo = (km - km_hi.astype(F32)).astype(BF16)
    gate = _dot_nt(km_hi, q2) + _dot_nt(km_lo, q2)
    row = lax.broadcasted_iota(jnp.int32, gate.shape, 0)
    gate = jnp.where(row < n_past, gate, -jnp.inf)
    sel = []
    for n in range(n_past):
        gn = gate[n:n + 1, :]
        ahead = jnp.where(gate > gn, 1.0, 0.0) + jnp.where((gate == gn) & (row < n), 1.0, 0.0)
        sel.append(jnp.where(jnp.sum(ahead, axis=0, keepdims=True) < MOBA_TOPK, 1.0, 0.0))
    return sel


def _attn_kernel(*refs, mode, n_blocks):
    if mode == "dilated":
        q_ref, k_ref, v_ref, w_ref, bias_ref, o_ref, wb_ref, vt_ref = refs
    else:
        q_ref, k_ref, v_ref, w_ref, o_ref, wb_ref, vt_ref, kmean_ref = refs
    wb_ref[...] = w_ref[...].astype(BF16)
    tb = ATT_BLOCK
    n_slots = q_ref.shape[0]
    n_steps = n_blocks // n_slots
    n_heads = q_ref.shape[2] // HEAD_DIM
    i = pl.program_id(2)
    lane = lax.broadcasted_iota(jnp.int32, (tb, LANES), 1)
    tile = lambda h: slice(h // 2 * LANES, (h // 2 + 1) * LANES)
    q2, qm = [], []
    for slot in range(n_slots):
        q2.append([])
        qm.append([])
        for t in range(n_heads // 2):
            qt = q_ref[slot, :, t * LANES:(t + 1) * LANES]
            q2[slot].append(qt * jnp.asarray(HEAD_DIM ** -0.5, BF16))
            ql = (qt.astype(F32) * (HEAD_DIM ** -0.5 * LOG2_E)).astype(BF16)
            zero = jnp.zeros_like(ql)
            qm[slot] += [jnp.where(lane < HEAD_DIM, ql, zero), jnp.where(lane >= HEAD_DIM, ql, zero)]

    @pl.when(i == 0)
    def _():
        for t in range(n_heads // 2):
            cols = slice(t * LANES, (t + 1) * LANES)
            for n in range(n_blocks):
                rows = slice(n * tb, (n + 1) * tb)
                vt_ref[cols, rows] = v_ref[rows, cols].astype(F32).T.astype(BF16)
        if mode == "moba":
            kmean_ref[...] = jnp.zeros_like(kmean_ref)
            for n in range(n_blocks):
                kb = k_ref[n * tb:(n + 1) * tb, :].astype(F32)
                kmean_ref[n:n + 1, :] = jnp.sum(kb, axis=0, keepdims=True) / tb

    if mode == "moba":
        key_pos = lax.broadcasted_iota(jnp.int32, (tb, tb), 0)
        qry_pos = lax.broadcasted_iota(jnp.int32, (tb, tb), 1)

    ones_rows = jnp.ones((ONES_ROWS, tb), BF16)
    key_rows = lambda j: slice(j * tb, (j + 1) * tb)

    def step(step_index):
        items = [(slot, step_index + slot * n_steps, h) for slot in range(n_slots) for h in range(n_heads)]
        sel = {}
        if mode == "moba":
            for slot, ib, h in items:
                if ib > MOBA_TOPK:
                    sel[slot, h] = _moba_selection(kmean_ref[:, tile(h)], q2[slot][h // 2], h % 2, ib)

        def score(item, j):
            slot, _, h = item
            return _dot_nt(k_ref[key_rows(j), tile(h)], qm[slot][h])

        def piece(item, j, s):
            slot, ib, h = item
            if mode == "dilated":
                off = (n_blocks - 1 - ib + j) * tb
                s = s + bias_ref[off:off + tb, :]
                m = jnp.max(s, axis=0, keepdims=True)
                return m, jnp.exp2(s - jnp.where(m == -jnp.inf, 0.0, m)).astype(BF16)
            if j == ib:
                s = jnp.where(key_pos <= qry_pos, s, -jnp.inf)
            m = jnp.max(s, axis=0, keepdims=True)
            p = jnp.exp2(s - m).astype(BF16)
            if j < ib and (slot, h) in sel:
                m = jnp.where(sel[slot, h][j] > 0.0, m, -jnp.inf)
            return m, p

        def weighted(item, j, p):
            h = item[2]
            vext = jnp.concatenate([vt_ref[h * HEAD_DIM:(h + 1) * HEAD_DIM, key_rows(j)], ones_rows], axis=0)
            return _dot(vext, p)

        n_keys = [ib + 1 for _, ib, _ in items]
        n_items = len(items)
        scores = [[score(items[0], j) for j in range(n_keys[0])]] + [[] for _ in range(n_items - 1)]
        pieces = [[] for _ in range(n_items)]
        pvs = [[] for _ in range(n_items)]
        for k in range(n_items + 1):
            for j in range(max(n_keys[max(k - 1, 0):k + 2])):
                if k + 1 < n_items and j < n_keys[k + 1]:
                    scores[k + 1].append(score(items[k + 1], j))
                if k >= 1 and j < n_keys[k - 1]:
                    pvs[k - 1].append(weighted(items[k - 1], j, pieces[k - 1][j][1]))
                if k < n_items and j < n_keys[k]:
                    pieces[k].append(piece(items[k], j, scores[k][j]))

        for slot in range(n_slots):
            outs = []
            for k in range(slot * n_heads, (slot + 1) * n_heads):
                m_all = functools.reduce(jnp.maximum, [m for m, _ in pieces[k]])
                acc = sum(jnp.exp2(m - m_all) * pv for (m, _), pv in zip(pieces[k], pvs[k]))
                outs.append(acc[:HEAD_DIM, :] / acc[HEAD_DIM:HEAD_DIM + 1, :])
            o_ref[slot] = jnp.concatenate(outs, axis=0).T.astype(o_ref.dtype)

    for step_index in range(n_steps):
        pl.when(i == step_index)(functools.partial(step, step_index))


def _dilated_bias(n_blocks):
    tb = ATT_BLOCK
    shape = (n_blocks, tb, tb)
    d = n_blocks - 1 - lax.broadcasted_iota(jnp.int32, shape, 0)
    c = lax.broadcasted_iota(jnp.int32, shape, 1)
    r = lax.broadcasted_iota(jnp.int32, shape, 2)
    delta = d * tb + r - c
    mult = jnp.zeros(shape, F32)
    for window, dilation in A_PATTERNS:
        hit = (delta >= 0) & (delta <= window) & (delta % dilation == 0)
        mult = mult + hit.astype(F32)
    return jnp.log2(mult).reshape(n_blocks * tb, tb)


def _attention(proj, q_col, mode, w_stack, layer, name):
    b, s, _ = proj.shape
    tb = ATT_BLOCK
    n_blocks = s // tb
    gw = ATT_HEADS_PER_STEP * HEAD_DIM
    n_groups = ATT_WIDTH // gw
    q_blk = q_col * LANES // gw
    k_blk, v_blk = q_blk + n_groups, q_blk + 2 * n_groups
    n_slots = ATT_QUERY_BLOCKS_PER_STEP
    n_steps = n_blocks // n_slots
    assert n_steps * n_slots == n_blocks
    proj_q = proj.reshape(b, n_slots, n_steps, tb, proj.shape[2])
    w_rows, w_cols = w_stack.shape[1:]
    n_slabs = b * n_groups * n_steps
    slab = w_rows // n_slabs
    assert slab * n_slabs == w_rows and slab % BF16_SUBLANES == 0
    slab_index = lambda bi, g, i: (bi * n_groups + g) * n_steps + i
    in_specs = [
        pl.BlockSpec((None, n_slots, None, tb, gw), lambda bi, g, i: (bi, 0, i, 0, q_blk + g)),
        pl.BlockSpec((None, s, gw), lambda bi, g, i: (bi, 0, k_blk + g)),
        pl.BlockSpec((None, s, gw), lambda bi, g, i: (bi, 0, v_blk + g)),
        pl.BlockSpec((None, slab, w_cols), lambda bi, g, i: (layer, slab_index(bi, g, i), 0)),
    ]
    args = [proj_q, proj, proj, w_stack]
    scratch = [pltpu.VMEM((gw, s), BF16)]
    if mode == "dilated":
        in_specs += [_resident_spec((s, tb), lambda bi, g, i: (0, 0))]
        args += [_dilated_bias(n_blocks)]
    else:
        scratch += [pltpu.VMEM((ONES_ROWS, gw), F32)]
    out, w_bf16 = pl.pallas_call(
        functools.partial(_attn_kernel, mode=mode, n_blocks=n_blocks),
        grid=(b, n_groups, n_steps),
        in_specs=in_specs,
        out_specs=[pl.BlockSpec((None, n_slots, None, tb, gw), lambda bi, g, i: (bi, 0, i, 0, g)),
                   pl.BlockSpec((slab, w_cols), lambda bi, g, i: (slab_index(bi, g, i), 0))],
        out_shape=[jax.ShapeDtypeStruct((b, n_slots, n_steps, tb, ATT_WIDTH), BF16),
                   jax.ShapeDtypeStruct((w_rows, w_cols), BF16)],
        scratch_shapes=scratch,
        compiler_params=_params("arbitrary", "arbitrary", "arbitrary"),
        name=name,
    )(*args)
    return out.reshape(b * s, ATT_WIDTH), w_bf16


def _spatial_gate(u, v, g_ref, b_ref, w_ref, bias_ref):
    u = _gelu_tanh(u)
    v = _layer_norm(_gelu_tanh(v), g_ref[...], b_ref[...])
    lane = lax.broadcasted_iota(jnp.int32, (C_CHUNK, LANES), 1)
    t_idx = lax.broadcasted_iota(jnp.int32, (C_CHUNK, C_CHUNK), 0)
    s_idx = lax.broadcasted_iota(jnp.int32, (C_CHUNK, C_CHUNK), 1)
    w = [jnp.where(s_idx <= t_idx, w_ref[g], 0.0).astype(BF16) for g in range(C_GROUPS)]
    per_tile = LANES // C_GROUP_DIM
    out_rows = []
    for ch in range(u.shape[0] // C_CHUNK):
        rows = slice(ch * C_CHUNK, (ch + 1) * C_CHUNK)
        out_cols = []
        for p in range(C_WIDTH // LANES):
            cols = slice(p * LANES, (p + 1) * LANES)
            v2 = v[rows, cols]
            sv = bias_ref[:, cols]
            for q in range(per_tile):
                in_group = (lane >= q * C_GROUP_DIM) & (lane < (q + 1) * C_GROUP_DIM)
                vq = jnp.where(in_group, v2, 0.0).astype(BF16)
                sv = sv + jnp.where(in_group, _dot(w[p * per_tile + q], vq), 0.0)
            out_cols.append((u[rows, cols] * sv).astype(BF16))
        out_rows.append(jnp.concatenate(out_cols, axis=1))
    return jnp.concatenate(out_rows, axis=0)


def _emit_norm(z, g_ref, b_ref, out_refs):
    y = _layer_norm(z, g_ref[...], b_ref[...])
    for o_ref in out_refs:
        o_ref[...] = y.astype(o_ref.dtype)


def _merge_kernel(oa_ref, ob_ref, u_ref, v_ref, ga_ref, gb_ref, gc_ref, x_ref,
                  cg_ref, cb_ref, cw_ref, cbias_ref,
                  wa_ref, wb_ref, wc_ref, wo_ref, g_ref, b_ref, *out_refs, alpha):
    def branch(y, w_ref, gate_ref):
        gate = 0.5 * jnp.tanh(0.5 * gate_ref[...].astype(F32)) + 0.5
        return gate * _dot(y, w_ref[...])

    yc = _spatial_gate(u_ref[...].astype(F32), v_ref[...].astype(F32), cg_ref, cb_ref, cw_ref, cbias_ref)
    merged = (branch(oa_ref[...], wa_ref, ga_ref) + branch(ob_ref[...], wb_ref, gb_ref)
              + branch(yc, wc_ref, gc_ref))
    mix = _dot(merged.astype(BF16), wo_ref[...])
    _emit_norm(alpha * x_ref[...] + mix, g_ref, b_ref, out_refs)


def _merge(oa, ob, proj2d, u_col, v_col, gate_col, x, c_ln_g, c_ln_b, c_ws, c_bs,
           wa, wb, wc, wo, ln_g, ln_b, alpha, tm):
    t, d = x.shape
    row = lambda i: (i, 0)
    const = lambda i: (0, 0)
    c_bias = jnp.repeat(c_bs.T, C_GROUP_DIM, axis=1)
    branch_spec = pl.BlockSpec((tm, ATT_WIDTH), row)
    uv_specs = [pl.BlockSpec((tm, C_WIDTH), lambda i, c=c: (i, c)) for c in (u_col, v_col)]
    gate_specs = [pl.BlockSpec((tm, d), lambda i, c=gate_col + n: (i, c)) for n in range(3)]
    c_specs = [pl.BlockSpec((1, C_WIDTH), const), pl.BlockSpec((1, C_WIDTH), const),
               pl.BlockSpec((C_GROUPS, C_CHUNK, C_CHUNK), lambda i: (0, 0, 0)),
               pl.BlockSpec((C_CHUNK, C_WIDTH), const)]
    w_br_spec = _resident_spec((ATT_WIDTH, d), const)
    vec_spec = pl.BlockSpec((1, d), const)
    return pl.pallas_call(
        functools.partial(_merge_kernel, alpha=alpha),
        grid=(t // tm,),
        in_specs=[branch_spec, branch_spec, *uv_specs, *gate_specs, pl.BlockSpec((tm, d), row), *c_specs,
                  w_br_spec, w_br_spec, w_br_spec, _resident_spec((d, d), const), vec_spec, vec_spec],
        out_specs=[pl.BlockSpec((tm, d), row), pl.BlockSpec((tm, d), row)],
        out_shape=[jax.ShapeDtypeStruct((t, d), F32), jax.ShapeDtypeStruct((t, d), BF16)],
        compiler_params=_params("parallel"),
        name="merge",
    )(oa, ob, proj2d, proj2d, proj2d, proj2d, proj2d, x,
      c_ln_g.reshape(1, -1), c_ln_b.reshape(1, -1), c_ws, c_bias,
      wa, wb, wc, wo, ln_g.reshape(1, -1), ln_b.reshape(1, -1))


def _cross_kernel(x_ref, xb_ref, k_ref, v_ref, wq_ref, wo_ref, g_ref, b_ref, *out_refs, alpha):
    q = (_dot(xb_ref[...], wq_ref[...]) * (X_HEAD_DIM ** -0.5)).astype(BF16)
    cols = [slice(h * X_HEAD_DIM, (h + 1) * X_HEAD_DIM) for h in range(X_HEADS)]
    scores = [_dot_nt(q[:, c], k_ref[:, c]) for c in cols]
    probs = [jnp.exp(s - jnp.max(s, axis=-1, keepdims=True)) for s in scores]
    heads = [(_dot(p.astype(BF16), v_ref[:, c]) / jnp.sum(p, axis=-1, keepdims=True)).astype(BF16)
             for p, c in zip(probs, cols)]
    cross = _dot(jnp.concatenate(heads, axis=-1), wo_ref[...])
    _emit_norm(alpha * x_ref[...] + cross, g_ref, b_ref, out_refs)


def _cross(x, xb, kv, wq, wo, ln_g, ln_b, alpha, batch, tm):
    t, d = x.shape
    s = t // batch
    m = kv.shape[1]
    x3, xb3 = x.reshape(batch, s, d), xb.reshape(batch, s, d)
    row = lambda bi, i: (bi, i, 0)
    const = lambda bi, i: (0, 0)
    outs = pl.pallas_call(
        functools.partial(_cross_kernel, alpha=alpha),
        grid=(batch, s // tm),
        in_specs=[pl.BlockSpec((None, tm, d), row), pl.BlockSpec((None, tm, d), row),
                  pl.BlockSpec((None, m, X_WIDTH), lambda bi, i: (bi, 0, 0)),
                  pl.BlockSpec((None, m, X_WIDTH), lambda bi, i: (bi, 0, 1)),
                  pl.BlockSpec((d, X_WIDTH), const), pl.BlockSpec((X_WIDTH, d), const),
                  pl.BlockSpec((1, d), const), pl.BlockSpec((1, d), const)],
        out_specs=[pl.BlockSpec((None, tm, d), row), pl.BlockSpec((None, tm, d), row)],
        out_shape=[jax.ShapeDtypeStruct((batch, s, d), F32), jax.ShapeDtypeStruct((batch, s, d), BF16)],
        compiler_params=_params("parallel", "parallel"),
        name="cross",
    )(x3, xb3, kv, kv, wq, wo, ln_g.reshape(1, -1), ln_b.reshape(1, -1))
    return outs[0].reshape(t, d), outs[1].reshape(t, d)


def _ffn_kernel(x_ref, xb_ref, w1_ref, w2_ref, g_ref, b_ref, *out_refs, alpha):
    acc_ref = out_refs[0]
    j = pl.program_id(1)

    def hidden_tile(first):
        h = jnp.maximum(_dot(xb_ref[...], w1_ref[...]), 0.0)
        part = _dot((h * h).astype(BF16), w2_ref[...])
        if first:
            acc_ref[...] = part
        else:
            acc_ref[...] += part

    pl.when(j == 0)(functools.partial(hidden_tile, True))
    pl.when(j > 0)(functools.partial(hidden_tile, False))

    @pl.when(j == pl.num_programs(1) - 1)
    def _():
        _emit_norm(alpha * x_ref[...] + acc_ref[...], g_ref, b_ref, out_refs)


def _ffn(x, xb, w1, w2, ln_g, ln_b, alpha, tm, tf, with_bf16):
    t, d = x.shape
    d_ff = w1.shape[1]
    row = lambda i, j: (i, 0)
    const = lambda i, j: (0, 0)
    out_dtypes = [F32, BF16] if with_bf16 else [F32]
    return pl.pallas_call(
        functools.partial(_ffn_kernel, alpha=alpha),
        grid=(t // tm, d_ff // tf),
        in_specs=[pl.BlockSpec((tm, d), row), pl.BlockSpec((tm, d), row),
                  pl.BlockSpec((d, tf), lambda i, j: (0, j)), pl.BlockSpec((tf, d), lambda i, j: (j, 0)),
                  pl.BlockSpec((1, d), const), pl.BlockSpec((1, d), const)],
        out_specs=[pl.BlockSpec((tm, d), row) for _ in out_dtypes],
        out_shape=[jax.ShapeDtypeStruct((t, d), dt) for dt in out_dtypes],
        compiler_params=_params("parallel", "arbitrary"),
        name="ffn",
    )(x, xb, w1, w2, ln_g.reshape(1, -1), ln_b.reshape(1, -1))


def kernel(x, mem, w_in, w_br_a, w_br_b, w_br_c, w_out, c_ln_g, c_ln_b, c_ws, c_bs,
           ln1_g, ln1_b, w_xq, w_xk, w_xv, w_xo, ln2_g, ln2_b, w_ff1, w_ff2, ln3_g, ln3_b):
    batch, seq, d = x.shape
    depth = w_in.shape[0]
    t = batch * seq
    alpha = (2 * depth) ** 0.25
    assert all(seq % window == 0 for window, _ in A_PATTERNS) and seq % ATT_BLOCK == 0
    assert w_in.shape[2] == 3 * ATT_WIDTH * 2 + 2 * C_WIDTH + 3 * d

    qa_blk, qb_blk = 0, 3 * ATT_WIDTH // LANES
    u_blk, v_blk = 6, 7
    gate_blk = (6 * ATT_WIDTH + 2 * C_WIDTH) // d

    xf = x.reshape(t, d)
    xb = xf
    mem_b = mem.reshape(-1, d).astype(BF16)
    for l in range(depth):
        proj2d = _matmul_f32w(xb, w_in, l, BF16, *TILES["in_proj"], "in_proj")
        proj = proj2d.reshape(batch, seq, -1)
        oa, w1 = _attention(proj, qa_blk, "dilated", w_ff1, l, "dilated_attn")
        ob, w2 = _attention(proj, qb_blk, "moba", w_ff2, l, "moba_attn")
        xf, xb = _merge(oa, ob, proj2d, u_blk, v_blk, gate_blk, xf,
                        c_ln_g[l], c_ln_b[l], c_ws[l], c_bs[l],
                        w_br_a[l].astype(BF16), w_br_b[l].astype(BF16), w_br_c[l].astype(BF16),
                        w_out[l].astype(BF16), ln1_g[l], ln1_b[l], alpha, TILES["merge"])
        w_kv = jnp.concatenate([w_xk[l], w_xv[l]], axis=1).astype(BF16)
        kv = _matmul(mem_b, w_kv, BF16, *TILES["mem_proj"], "mem_proj").reshape(batch, -1, 2 * X_WIDTH)
        xf, xb = _cross(xf, xb, kv, w_xq[l].astype(BF16), w_xo[l].astype(BF16),
                        ln2_g[l], ln2_b[l], alpha, batch, TILES["cross"])
        last = l == depth - 1
        outs = _ffn(xf, xb, w1, w2,
                    ln3_g[l], ln3_b[l], alpha, *TILES["ffn"], not last)
        xf = outs[0]
        xb = None if last else outs[1]
    return xf.reshape(batch, seq, d)
```

```python
import functools

import jax
import jax.numpy as jnp
from jax import lax
from jax.experimental import pallas as pl
from jax.experimental.pallas import tpu as pltpu

F32 = jnp.float32
BF16 = jnp.bfloat16

HEAD_DIM = 64
A_PATTERNS = ((128, 1), (512, 4), (2048, 16))
ATT_WIDTH = 512
MOBA_BLOCK = 256
MOBA_TOPK = 3
C_GROUPS = 8
C_GROUP_DIM = 64
C_WIDTH = C_GROUPS * C_GROUP_DIM
C_CHUNK = 128
X_HEADS = 4
X_HEAD_DIM = 128
X_WIDTH = X_HEADS * X_HEAD_DIM
LN_EPS = 1e-5
LOG2_E = 1.4426950408889634

LANES = 128
ATT_BLOCK = 256
ATT_HEADS_PER_STEP = 8
ATT_QUERY_BLOCKS_PER_STEP = 4
BF16_SUBLANES = 16
ONES_ROWS = BF16_SUBLANES
VMEM_LIMIT = 56 * 1024 * 1024

TILES = {"in_proj": (1024, 1024), "mem_proj": (1024, 1024), "merge": 512, "cross": 512, "ffn": (512, 1024)}


def _params(*sem):
    return pltpu.CompilerParams(dimension_semantics=sem, vmem_limit_bytes=VMEM_LIMIT)


def _resident_spec(shape, index_map):
    return pl.BlockSpec(shape, index_map, pipeline_mode=pl.Buffered(1))


def _layer_norm(z, g, b):
    mu = jnp.mean(z, axis=-1, keepdims=True)
    zc = z - mu
    var = jnp.mean(zc * zc, axis=-1, keepdims=True)
    return zc * lax.rsqrt(var + LN_EPS) * g + b


def _gelu_tanh(x):
    c = (2.0 / jnp.pi) ** 0.5
    return 0.5 * x * (1.0 + jnp.tanh(c * (x + 0.044715 * (x * x * x))))


def _dot(a, b):
    return jnp.dot(a, b, preferred_element_type=F32)


def _dot_nt(a, b):
    return lax.dot_general(a, b, (((1,), (1,)), ((), ())), preferred_element_type=F32)


def _matmul_kernel(a_ref, w_ref, o_ref):
    o_ref[...] = _dot(a_ref[...], w_ref[...]).astype(o_ref.dtype)


def _matmul(a, w, out_dtype, tm, tn, name):
    m, k = a.shape
    n = w.shape[1]
    tm, tn = min(tm, m), min(tn, n)
    assert m % tm == 0 and n % tn == 0
    return pl.pallas_call(
        _matmul_kernel,
        grid=(m // tm, n // tn),
        in_specs=[pl.BlockSpec((tm, k), lambda i, j: (i, 0)),
                  pl.BlockSpec((k, tn), lambda i, j: (0, j))],
        out_specs=pl.BlockSpec((tm, tn), lambda i, j: (i, j)),
        out_shape=jax.ShapeDtypeStruct((m, n), out_dtype),
        compiler_params=_params("parallel", "parallel"),
        name=name,
    )(a, w)


def _matmul_f32w_kernel(a_ref, w_ref, o_ref, wb_ref):
    @pl.when(pl.program_id(1) == 0)
    def _():
        wb_ref[...] = w_ref[...].astype(BF16)

    o_ref[...] = _dot(a_ref[...].astype(BF16), wb_ref[...]).astype(o_ref.dtype)


def _matmul_f32w(a, w_stack, layer, out_dtype, tm, tn, name):
    m, k = a.shape
    n = w_stack.shape[2]
    assert m % tm == 0 and n % tn == 0
    return pl.pallas_call(
        _matmul_f32w_kernel,
        grid=(n // tn, m // tm),
        in_specs=[pl.BlockSpec((tm, k), lambda j, i: (i, 0)),
                  pl.BlockSpec((None, k, tn), lambda j, i: (layer, 0, j))],
        out_specs=pl.BlockSpec((tm, tn), lambda j, i: (i, j)),
        out_shape=jax.ShapeDtypeStruct((m, n), out_dtype),
        scratch_shapes=[pltpu.VMEM((k, tn), BF16)],
        compiler_params=_params("parallel", "arbitrary"),
        name=name,
    )(a, w_stack)


def _moba_selection(kmean, q2, head, n_past):
    klane = lax.broadcasted_iota(jnp.int32, kmean.shape, 1)
    km = jnp.where((klane < HEAD_DIM) == (head == 0), kmean, 0.0)
    km_hi = km.astype(BF16)
    km_lo = (km - km_hi.astype(F32)).astype(BF16)
    gate = _dot_nt(km_hi, q2) + _dot_nt(km_lo, q2)
    row = lax.broadcasted_iota(jnp.int32, gate.shape, 0)
    gate = jnp.where(row < n_past, gate, -jnp.inf)
    sel = []
    for n in range(n_past):
        gn = gate[n:n + 1, :]
        ahead = jnp.where(gate > gn, 1.0, 0.0) + jnp.where((gate == gn) & (row < n), 1.0, 0.0)
        sel.append(jnp.where(jnp.sum(ahead, axis=0, keepdims=True) < MOBA_TOPK, 1.0, 0.0))
    return sel


def _attn_kernel(*refs, mode, n_blocks):
    if mode == "dilated":
        q_ref, k_ref, v_ref, w_ref, bias_ref, o_ref, wb_ref, vt_ref = refs
    else:
        q_ref, k_ref, v_ref, w_ref, o_ref, wb_ref, vt_ref, kmean_ref = refs
    wb_ref[...] = w_ref[...].astype(BF16)
    tb = ATT_BLOCK
    n_slots = q_ref.shape[0]
    n_steps = n_blocks // n_slots
    n_heads = q_ref.shape[2] // HEAD_DIM
    i = pl.program_id(2)
    lane = lax.broadcasted_iota(jnp.int32, (tb, LANES), 1)
    tile = lambda h: slice(h // 2 * LANES, (h // 2 + 1) * LANES)
    q2, qm = [], []
    for slot in range(n_slots):
        q2.append([])
        qm.append([])
        for t in range(n_heads // 2):
            qt = q_ref[slot, :, t * LANES:(t + 1) * LANES]
            q2[slot].append(qt * jnp.asarray(HEAD_DIM ** -0.5, BF16))
            ql = (qt.astype(F32) * (HEAD_DIM ** -0.5 * LOG2_E)).astype(BF16)
            zero = jnp.zeros_like(ql)
            qm[slot] += [jnp.where(lane < HEAD_DIM, ql, zero), jnp.where(lane >= HEAD_DIM, ql, zero)]

    @pl.when(i == 0)
    def _():
        for t in range(n_heads // 2):
            cols = slice(t * LANES, (t + 1) * LANES)
            for n in range(n_blocks):
                rows = slice(n * tb, (n + 1) * tb)
                vt_ref[cols, rows] = v_ref[rows, cols].astype(F32).T.astype(BF16)
        if mode == "moba":
            kmean_ref[...] = jnp.zeros_like(kmean_ref)
            for n in range(n_blocks):
                kb = k_ref[n * tb:(n + 1) * tb, :].astype(F32)
                kmean_ref[n:n + 1, :] = jnp.sum(kb, axis=0, keepdims=True) / tb

    if mode == "moba":
        key_pos = lax.broadcasted_iota(jnp.int32, (tb, tb), 0)
        qry_pos = lax.broadcasted_iota(jnp.int32, (tb, tb), 1)

    ones_rows = jnp.ones((ONES_ROWS, tb), BF16)
    key_rows = lambda j: slice(j * tb, (j + 1) * tb)

    def step(step_index):
        items = [(slot, step_index + slot * n_steps, h) for slot in range(n_slots) for h in range(n_heads)]
        sel = {}
        if mode == "moba":
            for slot, ib, h in items:
                if ib > MOBA_TOPK:
                    sel[slot, h] = _moba_selection(kmean_ref[:, tile(h)], q2[slot][h // 2], h % 2, ib)

        def score(item, j):
            slot, _, h = item
            return _dot_nt(k_ref[key_rows(j), tile(h)], qm[slot][h])

        def piece(item, j, s):
            slot, ib, h = item
            if mode == "dilated":
                off = (n_blocks - 1 - ib + j) * tb
                s = s + bias_ref[off:off + tb, :]
                m = jnp.max(s, axis=0, keepdims=True)
                return m, jnp.exp2(s - jnp.where(m == -jnp.inf, 0.0, m)).astype(BF16)
            if j == ib:
                s = jnp.where(key_pos <= qry_pos, s, -jnp.inf)
            m = jnp.max(s, axis=0, keepdims=True)
            p = jnp.exp2(s - m).astype(BF16)
            if j < ib and (slot, h) in sel:
                m = jnp.where(sel[slot, h][j] > 0.0, m, -jnp.inf)
            return m, p

        def weighted(item, j, p):
            h = item[2]
            vext = jnp.concatenate([vt_ref[h * HEAD_DIM:(h + 1) * HEAD_DIM, key_rows(j)], ones_rows], axis=0)
            return _dot(vext, p)

        n_keys = [ib + 1 for _, ib, _ in items]
        n_items = len(items)
        scores = [[score(items[0], j) for j in range(n_keys[0])]] + [[] for _ in range(n_items - 1)]
        pieces = [[] for _ in range(n_items)]
        pvs = [[] for _ in range(n_items)]
        for k in range(n_items + 1):
            for j in range(max(n_keys[max(k - 1, 0):k + 2])):
                if k + 1 < n_items and j < n_keys[k + 1]:
                    scores[k + 1].append(score(items[k + 1], j))
                if k >= 1 and j < n_keys[k - 1]:
                    pvs[k - 1].append(weighted(items[k - 1], j, pieces[k - 1][j][1]))
                if k < n_items and j < n_keys[k]:
                    pieces[k].append(piece(items[k], j, scores[k][j]))

        for slot in range(n_slots):
            outs = []
            for k in range(slot * n_heads, (slot + 1) * n_heads):
                m_all = functools.reduce(jnp.maximum, [m for m, _ in pieces[k]])
                acc = sum(jnp.exp2(m - m_all) * pv for (m, _), pv in zip(pieces[k], pvs[k]))
                outs.append(acc[:HEAD_DIM, :] / acc[HEAD_DIM:HEAD_DIM + 1, :])
            o_ref[slot] = jnp.concatenate(outs, axis=0).T.astype(o_ref.dtype)

    for step_index in range(n_steps):
        pl.when(i == step_index)(functools.partial(step, step_index))


def _dilated_bias(n_blocks):
    tb = ATT_BLOCK
    shape = (n_blocks, tb, tb)
    d = n_blocks - 1 - lax.broadcasted_iota(jnp.int32, shape, 0)
    c = lax.broadcasted_iota(jnp.int32, shape, 1)
    r = lax.broadcasted_iota(jnp.int32, shape, 2)
    delta = d * tb + r - c
    mult = jnp.zeros(shape, F32)
    for window, dilation in A_PATTERNS:
        hit = (delta >= 0) & (delta <= window) & (delta % dilation == 0)
        mult = mult + hit.astype(F32)
    return jnp.log2(mult).reshape(n_blocks * tb, tb)


def _attention(proj, q_col, mode, w_stack, layer, name):
    b, s, _ = proj.shape
    tb = ATT_BLOCK
    n_blocks = s // tb
    gw = ATT_HEADS_PER_STEP * HEAD_DIM
    n_groups = ATT_WIDTH // gw
    q_blk = q_col * LANES // gw
    k_blk, v_blk = q_blk + n_groups, q_blk + 2 * n_groups
    n_slots = ATT_QUERY_BLOCKS_PER_STEP
    n_steps = n_blocks // n_slots
    assert n_steps * n_slots == n_blocks
    proj_q = proj.reshape(b, n_slots, n_steps, tb, proj.shape[2])
    w_rows, w_cols = w_stack.shape[1:]
    n_slabs = b * n_groups * n_steps
    slab = w_rows // n_slabs
    assert slab * n_slabs == w_rows and slab % BF16_SUBLANES == 0
    slab_index = lambda bi, g, i: (bi * n_groups + g) * n_steps + i
    in_specs = [
        pl.BlockSpec((None, n_slots, None, tb, gw), lambda bi, g, i: (bi, 0, i, 0, q_blk + g)),
        pl.BlockSpec((None, s, gw), lambda bi, g, i: (bi, 0, k_blk + g)),
        pl.BlockSpec((None, s, gw), lambda bi, g, i: (bi, 0, v_blk + g)),
        pl.BlockSpec((None, slab, w_cols), lambda bi, g, i: (layer, slab_index(bi, g, i), 0)),
    ]
    args = [proj_q, proj, proj, w_stack]
    scratch = [pltpu.VMEM((gw, s), BF16)]
    if mode == "dilated":
        in_specs += [_resident_spec((s, tb), lambda bi, g, i: (0, 0))]
        args += [_dilated_bias(n_blocks)]
    else:
        scratch += [pltpu.VMEM((ONES_ROWS, gw), F32)]
    out, w_bf16 = pl.pallas_call(
        functools.partial(_attn_kernel, mode=mode, n_blocks=n_blocks),
        grid=(b, n_groups, n_steps),
        in_specs=in_specs,
        out_specs=[pl.BlockSpec((None, n_slots, None, tb, gw), lambda bi, g, i: (bi, 0, i, 0, g)),
                   pl.BlockSpec((slab, w_cols), lambda bi, g, i: (slab_index(bi, g, i), 0))],
        out_shape=[jax.ShapeDtypeStruct((b, n_slots, n_steps, tb, ATT_WIDTH), BF16),
                   jax.ShapeDtypeStruct((w_rows, w_cols), BF16)],
        scratch_shapes=scratch,
        compiler_params=_params("arbitrary", "arbitrary", "arbitrary"),
        name=name,
    )(*args)
    return out.reshape(b * s, ATT_WIDTH), w_bf16


def _spatial_gate(u, v, g_ref, b_ref, w_ref, bias_ref):
    u = _gelu_tanh(u)
    v = _layer_norm(_gelu_tanh(v), g_ref[...], b_ref[...])
    lane = lax.broadcasted_iota(jnp.int32, (C_CHUNK, LANES), 1)
    t_idx = lax.broadcasted_iota(jnp.int32, (C_CHUNK, C_CHUNK), 0)
    s_idx = lax.broadcasted_iota(jnp.int32, (C_CHUNK, C_CHUNK), 1)
    w = [jnp.where(s_idx <= t_idx, w_ref[g], 0.0).astype(BF16) for g in range(C_GROUPS)]
    per_tile = LANES // C_GROUP_DIM
    out_rows = []
    for ch in range(u.shape[0] // C_CHUNK):
        rows = slice(ch * C_CHUNK, (ch + 1) * C_CHUNK)
        out_cols = []
        for p in range(C_WIDTH // LANES):
            cols = slice(p * LANES, (p + 1) * LANES)
            v2 = v[rows, cols]
            sv = bias_ref[:, cols]
            for q in range(per_tile):
                in_group = (lane >= q * C_GROUP_DIM) & (lane < (q + 1) * C_GROUP_DIM)
                vq = jnp.where(in_group, v2, 0.0).astype(BF16)
                sv = sv + jnp.where(in_group, _dot(w[p * per_tile + q], vq), 0.0)
            out_cols.append((u[rows, cols] * sv).astype(BF16))
        out_rows.append(jnp.concatenate(out_cols, axis=1))
    return jnp.concatenate(out_rows, axis=0)


def _emit_norm(z, g_ref, b_ref, out_refs):
    y = _layer_norm(z, g_ref[...], b_ref[...])
    for o_ref in out_refs:
        o_ref[...] = y.astype(o_ref.dtype)


def _merge_kernel(oa_ref, ob_ref, u_ref, v_ref, ga_ref, gb_ref, gc_ref, x_ref,
                  cg_ref, cb_ref, cw_ref, cbias_ref,
                  wa_ref, wb_ref, wc_ref, wo_ref, g_ref, b_ref, *out_refs, alpha):
    def branch(y, w_ref, gate_ref):
        gate = 0.5 * jnp.tanh(0.5 * gate_ref[...].astype(F32)) + 0.5
        return gate * _dot(y, w_ref[...])

    yc = _spatial_gate(u_ref[...].astype(F32), v_ref[...].astype(F32), cg_ref, cb_ref, cw_ref, cbias_ref)
    merged = (branch(oa_ref[...], wa_ref, ga_ref) + branch(ob_ref[...], wb_ref, gb_ref)
              + branch(yc, wc_ref, gc_ref))
    mix = _dot(merged.astype(BF16), wo_ref[...])
    _emit_norm(alpha * x_ref[...] + mix, g_ref, b_ref, out_refs)


def _merge(oa, ob, proj2d, u_col, v_col, gate_col, x, c_ln_g, c_ln_b, c_ws, c_bs,
           wa, wb, wc, wo, ln_g, ln_b, alpha, tm):
    t, d = x.shape
    row = lambda i: (i, 0)
    const = lambda i: (0, 0)
    c_bias = jnp.repeat(c_bs.T, C_GROUP_DIM, axis=1)
    branch_spec = pl.BlockSpec((tm, ATT_WIDTH), row)
    uv_specs = [pl.BlockSpec((tm, C_WIDTH), lambda i, c=c: (i, c)) for c in (u_col, v_col)]
    gate_specs = [pl.BlockSpec((tm, d), lambda i, c=gate_col + n: (i, c)) for n in range(3)]
    c_specs = [pl.BlockSpec((1, C_WIDTH), const), pl.BlockSpec((1, C_WIDTH), const),
               pl.BlockSpec((C_GROUPS, C_CHUNK, C_CHUNK), lambda i: (0, 0, 0)),
               pl.BlockSpec((C_CHUNK, C_WIDTH), const)]
    w_br_spec = _resident_spec((ATT_WIDTH, d), const)
    vec_spec = pl.BlockSpec((1, d), const)
    return pl.pallas_call(
        functools.partial(_merge_kernel, alpha=alpha),
        grid=(t // tm,),
        in_specs=[branch_spec, branch_spec, *uv_specs, *gate_specs, pl.BlockSpec((tm, d), row), *c_specs,
                  w_br_spec, w_br_spec, w_br_spec, _resident_spec((d, d), const), vec_spec, vec_spec],
        out_specs=[pl.BlockSpec((tm, d), row), pl.BlockSpec((tm, d), row)],
        out_shape=[jax.ShapeDtypeStruct((t, d), F32), jax.ShapeDtypeStruct((t, d), BF16)],
        compiler_params=_params("parallel"),
        name="merge",
    )(oa, ob, proj2d, proj2d, proj2d, proj2d, proj2d, x,
      c_ln_g.reshape(1, -1), c_ln_b.reshape(1, -1), c_ws, c_bias,
      wa, wb, wc, wo, ln_g.reshape(1, -1), ln_b.reshape(1, -1))


def _cross_kernel(x_ref, xb_ref, k_ref, v_ref, wq_ref, wo_ref, g_ref, b_ref, *out_refs, alpha):
    q = (_dot(xb_ref[...], wq_ref[...]) * (X_HEAD_DIM ** -0.5)).astype(BF16)
    cols = [slice(h * X_HEAD_DIM, (h + 1) * X_HEAD_DIM) for h in range(X_HEADS)]
    scores = [_dot_nt(q[:, c], k_ref[:, c]) for c in cols]
    probs = [jnp.exp(s - jnp.max(s, axis=-1, keepdims=True)) for s in scores]
    heads = [(_dot(p.astype(BF16), v_ref[:, c]) / jnp.sum(p, axis=-1, keepdims=True)).astype(BF16)
             for p, c in zip(probs, cols)]
    cross = _dot(jnp.concatenate(heads, axis=-1), wo_ref[...])
    _emit_norm(alpha * x_ref[...] + cross, g_ref, b_ref, out_refs)


def _cross(x, xb, kv, wq, wo, ln_g, ln_b, alpha, batch, tm):
    t, d = x.shape
    s = t // batch
    m = kv.shape[1]
    x3, xb3 = x.reshape(batch, s, d), xb.reshape(batch, s, d)
    row = lambda bi, i: (bi, i, 0)
    const = lambda bi, i: (0, 0)
    outs = pl.pallas_call(
        functools.partial(_cross_kernel, alpha=alpha),
        grid=(batch, s // tm),
        in_specs=[pl.BlockSpec((None, tm, d), row), pl.BlockSpec((None, tm, d), row),
                  pl.BlockSpec((None, m, X_WIDTH), lambda bi, i: (bi, 0, 0)),
                  pl.BlockSpec((None, m, X_WIDTH), lambda bi, i: (bi, 0, 1)),
                  pl.BlockSpec((d, X_WIDTH), const), pl.BlockSpec((X_WIDTH, d), const),
                  pl.BlockSpec((1, d), const), pl.BlockSpec((1, d), const)],
        out_specs=[pl.BlockSpec((None, tm, d), row), pl.BlockSpec((None, tm, d), row)],
        out_shape=[jax.ShapeDtypeStruct((batch, s, d), F32), jax.ShapeDtypeStruct((batch, s, d), BF16)],
        compiler_params=_params("parallel", "parallel"),
        name="cross",
    )(x3, xb3, kv, kv, wq, wo, ln_g.reshape(1, -1), ln_b.reshape(1, -1))
    return outs[0].reshape(t, d), outs[1].reshape(t, d)


def _ffn_kernel(x_ref, xb_ref, w1_ref, w2_ref, g_ref, b_ref, *out_refs, alpha):
    acc_ref = out_refs[0]
    j = pl.program_id(1)

    @pl.when(j == 0)
    def _():
        acc_ref[...] = jnp.zeros_like(acc_ref)

    h = jnp.maximum(_dot(xb_ref[...], w1_ref[...]), 0.0)
    acc_ref[...] += _dot((h * h).astype(BF16), w2_ref[...])

    @pl.when(j == pl.num_programs(1) - 1)
    def _():
        _emit_norm(alpha * x_ref[...] + acc_ref[...], g_ref, b_ref, out_refs)


def _ffn(x, xb, w1, w2, ln_g, ln_b, alpha, tm, tf, with_bf16):
    t, d = x.shape
    d_ff = w1.shape[1]
    row = lambda i, j: (i, 0)
    const = lambda i, j: (0, 0)
    out_dtypes = [F32, BF16] if with_bf16 else [F32]
    return pl.pallas_call(
        functools.partial(_ffn_kernel, alpha=alpha),
        grid=(t // tm, d_ff // tf),
        in_specs=[pl.BlockSpec((tm, d), row), pl.BlockSpec((tm, d), row),
                  pl.BlockSpec((d, tf), lambda i, j: (0, j)), pl.BlockSpec((tf, d), lambda i, j: (j, 0)),
                  pl.BlockSpec((1, d), const), pl.BlockSpec((1, d), const)],
        out_specs=[pl.BlockSpec((tm, d), row) for _ in out_dtypes],
        out_shape=[jax.ShapeDtypeStruct((t, d), dt) for dt in out_dtypes],
        compiler_params=_params("parallel", "arbitrary"),
        name="ffn",
    )(x, xb, w1, w2, ln_g.reshape(1, -1), ln_b.reshape(1, -1))


def kernel(x, mem, w_in, w_br_a, w_br_b, w_br_c, w_out, c_ln_g, c_ln_b, c_ws, c_bs,
           ln1_g, ln1_b, w_xq, w_xk, w_xv, w_xo, ln2_g, ln2_b, w_ff1, w_ff2, ln3_g, ln3_b):
    batch, seq, d = x.shape
    depth = w_in.shape[0]
    t = batch * seq
    alpha = (2 * depth) ** 0.25
    assert all(seq % window == 0 for window, _ in A_PATTERNS) and seq % ATT_BLOCK == 0
    assert w_in.shape[2] == 3 * ATT_WIDTH * 2 + 2 * C_WIDTH + 3 * d

    qa_blk, qb_blk = 0, 3 * ATT_WIDTH // LANES
    u_blk, v_blk = 6, 7
    gate_blk = (6 * ATT_WIDTH + 2 * C_WIDTH) // d

    xf = x.reshape(t, d)
    xb = xf
    mem_b = mem.reshape(-1, d).astype(BF16)
    for l in range(depth):
        proj2d = _matmul_f32w(xb, w_in, l, BF16, *TILES["in_proj"], "in_proj")
        proj = proj2d.reshape(batch, seq, -1)
        oa, w1 = _attention(proj, qa_blk, "dilated", w_ff1, l, "dilated_attn")
        ob, w2 = _attention(proj, qb_blk, "moba", w_ff2, l, "moba_attn")
        xf, xb = _merge(oa, ob, proj2d, u_blk, v_blk, gate_blk, xf,
                        c_ln_g[l], c_ln_b[l], c_ws[l], c_bs[l],
                        w_br_a[l].astype(BF16), w_br_b[l].astype(BF16), w_br_c[l].astype(BF16),
                        w_out[l].astype(BF16), ln1_g[l], ln1_b[l], alpha, TILES["merge"])
        w_kv = jnp.concatenate([w_xk[l], w_xv[l]], axis=1).astype(BF16)
        kv = _matmul(mem_b, w_kv, BF16, *TILES["mem_proj"], "mem_proj").reshape(batch, -1, 2 * X_WIDTH)
        xf, xb = _cross(xf, xb, kv, w_xq[l].astype(BF16), w_xo[l].astype(BF16),
                        ln2_g[l], ln2_b[l], alpha, batch, TILES["cross"])
        last = l == depth - 1
        outs = _ffn(xf, xb, w1, w2,
                    ln3_g[l], ln3_b[l], alpha, *TILES["ffn"], not last)
        xf = outs[0]
        xb = None if last else outs[1]
    return xf.reshape(batch, seq, d)
```

```python
import functools

import jax
import jax.numpy as jnp
from jax import lax
from jax.experimental import pallas as pl
from jax.experimental.pallas import tpu as pltpu

F32 = jnp.float32
BF16 = jnp.bfloat16

HEAD_DIM = 64
A_PATTERNS = ((128, 1), (512, 4), (2048, 16))
ATT_WIDTH = 512
MOBA_BLOCK = 256
MOBA_TOPK = 3
C_GROUPS = 8
C_GROUP_DIM = 64
C_WIDTH = C_GROUPS * C_GROUP_DIM
C_CHUNK = 128
X_HEADS = 4
X_HEAD_DIM = 128
X_WIDTH = X_HEADS * X_HEAD_DIM
LN_EPS = 1e-5
LOG2_E = 1.4426950408889634

LANES = 128
ATT_BLOCK = 256
ATT_HEADS_PER_STEP = 8
ATT_QUERY_BLOCKS_PER_STEP = 4
BF16_SUBLANES = 16
ONES_ROWS = BF16_SUBLANES
VMEM_LIMIT = 56 * 1024 * 1024

TILES = {"in_proj": (1024, 1024), "mem_proj": (1024, 1024), "merge": 256, "cross": 512, "ffn": (512, 1024)}


def _params(*sem):
    return pltpu.CompilerParams(dimension_semantics=sem, vmem_limit_bytes=VMEM_LIMIT)


def _resident_spec(shape, index_map):
    return pl.BlockSpec(shape, index_map, pipeline_mode=pl.Buffered(1))


def _layer_norm(z, g, b):
    mu = jnp.mean(z, axis=-1, keepdims=True)
    zc = z - mu
    var = jnp.mean(zc * zc, axis=-1, keepdims=True)
    return zc * lax.rsqrt(var + LN_EPS) * g + b


def _gelu_tanh(x):
    c = (2.0 / jnp.pi) ** 0.5
    return 0.5 * x * (1.0 + jnp.tanh(c * (x + 0.044715 * (x * x * x))))


def _dot(a, b):
    return jnp.dot(a, b, preferred_element_type=F32)


def _dot_nt(a, b):
    return lax.dot_general(a, b, (((1,), (1,)), ((), ())), preferred_element_type=F32)


def _matmul_kernel(a_ref, w_ref, o_ref):
    o_ref[...] = _dot(a_ref[...], w_ref[...]).astype(o_ref.dtype)


def _matmul(a, w, out_dtype, tm, tn, name):
    m, k = a.shape
    n = w.shape[1]
    tm, tn = min(tm, m), min(tn, n)
    assert m % tm == 0 and n % tn == 0
    return pl.pallas_call(
        _matmul_kernel,
        grid=(m // tm, n // tn),
        in_specs=[pl.BlockSpec((tm, k), lambda i, j: (i, 0)),
                  pl.BlockSpec((k, tn), lambda i, j: (0, j))],
        out_specs=pl.BlockSpec((tm, tn), lambda i, j: (i, j)),
        out_shape=jax.ShapeDtypeStruct((m, n), out_dtype),
        compiler_params=_params("parallel", "parallel"),
        name=name,
    )(a, w)


def _matmul_f32w_kernel(a_ref, w_ref, o_ref, wb_ref):
    @pl.when(pl.program_id(1) == 0)
    def _():
        wb_ref[...] = w_ref[...].astype(BF16)

    o_ref[...] = _dot(a_ref[...].astype(BF16), wb_ref[...]).astype(o_ref.dtype)


def _matmul_f32w(a, w_stack, layer, out_dtype, tm, tn, name):
    m, k = a.shape
    n = w_stack.shape[2]
    assert m % tm == 0 and n % tn == 0
    return pl.pallas_call(
        _matmul_f32w_kernel,
        grid=(n // tn, m // tm),
        in_specs=[pl.BlockSpec((tm, k), lambda j, i: (i, 0)),
                  pl.BlockSpec((None, k, tn), lambda j, i: (layer, 0, j))],
        out_specs=pl.BlockSpec((tm, tn), lambda j, i: (i, j)),
        out_shape=jax.ShapeDtypeStruct((m, n), out_dtype),
        scratch_shapes=[pltpu.VMEM((k, tn), BF16)],
        compiler_params=_params("parallel", "arbitrary"),
        name=name,
    )(a, w_stack)


def _moba_selection(kmean, q2, head, n_past):
    klane = lax.broadcasted_iota(jnp.int32, kmean.shape, 1)
    km = jnp.where((klane < HEAD_DIM) == (head == 0), kmean, 0.0)
    km_hi = km.astype(BF16)
    km_lo = (km - km_hi.astype(F32)).astype(BF16)
    gate = _dot_nt(km_hi, q2) + _dot_nt(km_lo, q2)
    row = lax.broadcasted_iota(jnp.int32, gate.shape, 0)
    gate = jnp.where(row < n_past, gate, -jnp.inf)
    sel = []
    for n in range(n_past):
        gn = gate[n:n + 1, :]
        ahead = jnp.where(gate > gn, 1.0, 0.0) + jnp.where((gate == gn) & (row < n), 1.0, 0.0)
        sel.append(jnp.where(jnp.sum(ahead, axis=0, keepdims=True) < MOBA_TOPK, 1.0, 0.0))
    return sel


def _attn_kernel(*refs, mode, n_blocks):
    if mode == "dilated":
        q_ref, k_ref, v_ref, w_ref, bias_ref, o_ref, wb_ref, vt_ref = refs
    else:
        q_ref, k_ref, v_ref, w_ref, o_ref, wb_ref, vt_ref, kmean_ref = refs
    wb_ref[...] = w_ref[...].astype(BF16)
    tb = ATT_BLOCK
    n_slots = q_ref.shape[0]
    n_steps = n_blocks // n_slots
    n_heads = q_ref.shape[2] // HEAD_DIM
    i = pl.program_id(2)
    lane = lax.broadcasted_iota(jnp.int32, (tb, LANES), 1)
    tile = lambda h: slice(h // 2 * LANES, (h // 2 + 1) * LANES)
    q2, qm = [], []
    for slot in range(n_slots):
        q2.append([])
        qm.append([])
        for t in range(n_heads // 2):
            qt = q_ref[slot, :, t * LANES:(t + 1) * LANES]
            q2[slot].append(qt * jnp.asarray(HEAD_DIM ** -0.5, BF16))
            ql = (qt.astype(F32) * (HEAD_DIM ** -0.5 * LOG2_E)).astype(BF16)
            zero = jnp.zeros_like(ql)
            qm[slot] += [jnp.where(lane < HEAD_DIM, ql, zero), jnp.where(lane >= HEAD_DIM, ql, zero)]

    @pl.when(i == 0)
    def _():
        for t in range(n_heads // 2):
            cols = slice(t * LANES, (t + 1) * LANES)
            for n in range(n_blocks):
                rows = slice(n * tb, (n + 1) * tb)
                vt_ref[cols, rows] = v_ref[rows, cols].astype(F32).T.astype(BF16)
        if mode == "moba":
            kmean_ref[...] = jnp.zeros_like(kmean_ref)
            for n in range(n_blocks):
                kb = k_ref[n * tb:(n + 1) * tb, :].astype(F32)
                kmean_ref[n:n + 1, :] = jnp.sum(kb, axis=0, keepdims=True) / tb

    if mode == "moba":
        key_pos = lax.broadcasted_iota(jnp.int32, (tb, tb), 0)
        qry_pos = lax.broadcasted_iota(jnp.int32, (tb, tb), 1)

    ones_rows = jnp.ones((ONES_ROWS, tb), BF16)
    key_rows = lambda j: slice(j * tb, (j + 1) * tb)

    def step(step_index):
        items = [(slot, step_index + slot * n_steps, h) for slot in range(n_slots) for h in range(n_heads)]
        sel = {}
        if mode == "moba":
            for slot, ib, h in items:
                if ib > MOBA_TOPK:
                    sel[slot, h] = _moba_selection(kmean_ref[:, tile(h)], q2[slot][h // 2], h % 2, ib)

        def score(item, j):
            slot, _, h = item
            return _dot_nt(k_ref[key_rows(j), tile(h)], qm[slot][h])

        def piece(item, j, s):
            slot, ib, h = item
            if mode == "dilated":
                off = (n_blocks - 1 - ib + j) * tb
                s = s + bias_ref[off:off + tb, :]
                m = jnp.max(s, axis=0, keepdims=True)
                return m, jnp.exp2(s - jnp.where(m == -jnp.inf, 0.0, m)).astype(BF16)
            if j == ib:
                s = jnp.where(key_pos <= qry_pos, s, -jnp.inf)
            m = jnp.max(s, axis=0, keepdims=True)
            p = jnp.exp2(s - m).astype(BF16)
            if j < ib and (slot, h) in sel:
                m = jnp.where(sel[slot, h][j] > 0.0, m, -jnp.inf)
            return m, p

        def weighted(item, j, p):
            h = item[2]
            vext = jnp.concatenate([vt_ref[h * HEAD_DIM:(h + 1) * HEAD_DIM, key_rows(j)], ones_rows], axis=0)
            return _dot(vext, p)

        n_keys = [ib + 1 for _, ib, _ in items]
        n_items = len(items)
        scores = [[score(items[0], j) for j in range(n_keys[0])]] + [[] for _ in range(n_items - 1)]
        pieces = [[] for _ in range(n_items)]
        pvs = [[] for _ in range(n_items)]
        for k in range(n_items + 1):
            for j in range(max(n_keys[max(k - 1, 0):k + 2])):
                if k + 1 < n_items and j < n_keys[k + 1]:
                    scores[k + 1].append(score(items[k + 1], j))
                if k >= 1 and j < n_keys[k - 1]:
                    pvs[k - 1].append(weighted(items[k - 1], j, pieces[k - 1][j][1]))
                if k < n_items and j < n_keys[k]:
                    pieces[k].append(piece(items[k], j, scores[k][j]))

        for slot in range(n_slots):
            outs = []
            for k in range(slot * n_heads, (slot + 1) * n_heads):
                m_all = functools.reduce(jnp.maximum, [m for m, _ in pieces[k]])
                acc = sum(jnp.exp2(m - m_all) * pv for (m, _), pv in zip(pieces[k], pvs[k]))
                outs.append(acc[:HEAD_DIM, :] / acc[HEAD_DIM:HEAD_DIM + 1, :])
            o_ref[slot] = jnp.concatenate(outs, axis=0).T.astype(o_ref.dtype)

    for step_index in range(n_steps):
        pl.when(i == step_index)(functools.partial(step, step_index))


def _dilated_bias(n_blocks):
    tb = ATT_BLOCK
    shape = (n_blocks, tb, tb)
    d = n_blocks - 1 - lax.broadcasted_iota(jnp.int32, shape, 0)
    c = lax.broadcasted_iota(jnp.int32, shape, 1)
    r = lax.broadcasted_iota(jnp.int32, shape, 2)
    delta = d * tb + r - c
    mult = jnp.zeros(shape, F32)
    for window, dilation in A_PATTERNS:
        hit = (delta >= 0) & (delta <= window) & (delta % dilation == 0)
        mult = mult + hit.astype(F32)
    return jnp.log2(mult).reshape(n_blocks * tb, tb)


def _attention(proj, q_col, mode, w_stack, layer, name):
    b, s, _ = proj.shape
    tb = ATT_BLOCK
    n_blocks = s // tb
    gw = ATT_HEADS_PER_STEP * HEAD_DIM
    n_groups = ATT_WIDTH // gw
    q_blk = q_col * LANES // gw
    k_blk, v_blk = q_blk + n_groups, q_blk + 2 * n_groups
    n_slots = ATT_QUERY_BLOCKS_PER_STEP
    n_steps = n_blocks // n_slots
    assert n_steps * n_slots == n_blocks
    proj_q = proj.reshape(b, n_slots, n_steps, tb, proj.shape[2])
    w_rows, w_cols = w_stack.shape[1:]
    n_slabs = b * n_groups * n_steps
    slab = w_rows // n_slabs
    assert slab * n_slabs == w_rows and slab % BF16_SUBLANES == 0
    slab_index = lambda bi, g, i: (bi * n_groups + g) * n_steps + i
    in_specs = [
        pl.BlockSpec((None, n_slots, None, tb, gw), lambda bi, g, i: (bi, 0, i, 0, q_blk + g)),
        pl.BlockSpec((None, s, gw), lambda bi, g, i: (bi, 0, k_blk + g)),
        pl.BlockSpec((None, s, gw), lambda bi, g, i: (bi, 0, v_blk + g)),
        pl.BlockSpec((None, slab, w_cols), lambda bi, g, i: (layer, slab_index(bi, g, i), 0)),
    ]
    args = [proj_q, proj, proj, w_stack]
    scratch = [pltpu.VMEM((gw, s), BF16)]
    if mode == "dilated":
        in_specs += [_resident_spec((s, tb), lambda bi, g, i: (0, 0))]
        args += [_dilated_bias(n_blocks)]
    else:
        scratch += [pltpu.VMEM((ONES_ROWS, gw), F32)]
    out, w_bf16 = pl.pallas_call(
        functools.partial(_attn_kernel, mode=mode, n_blocks=n_blocks),
        grid=(b, n_groups, n_steps),
        in_specs=in_specs,
        out_specs=[pl.BlockSpec((None, n_slots, None, tb, gw), lambda bi, g, i: (bi, 0, i, 0, g)),
                   pl.BlockSpec((slab, w_cols), lambda bi, g, i: (slab_index(bi, g, i), 0))],
        out_shape=[jax.ShapeDtypeStruct((b, n_slots, n_steps, tb, ATT_WIDTH), BF16),
                   jax.ShapeDtypeStruct((w_rows, w_cols), BF16)],
        scratch_shapes=scratch,
        compiler_params=_params("arbitrary", "arbitrary", "arbitrary"),
        name=name,
    )(*args)
    return out.reshape(b * s, ATT_WIDTH), w_bf16


def _spatial_gate(u, v, g_ref, b_ref, w_ref, bias_ref):
    u = _gelu_tanh(u)
    v = _layer_norm(_gelu_tanh(v), g_ref[...], b_ref[...])
    lane = lax.broadcasted_iota(jnp.int32, (C_CHUNK, LANES), 1)
    t_idx = lax.broadcasted_iota(jnp.int32, (C_CHUNK, C_CHUNK), 0)
    s_idx = lax.broadcasted_iota(jnp.int32, (C_CHUNK, C_CHUNK), 1)
    w = [jnp.where(s_idx <= t_idx, w_ref[g], 0.0).astype(BF16) for g in range(C_GROUPS)]
    per_tile = LANES // C_GROUP_DIM
    out_rows = []
    for ch in range(u.shape[0] // C_CHUNK):
        rows = slice(ch * C_CHUNK, (ch + 1) * C_CHUNK)
        out_cols = []
        for p in range(C_WIDTH // LANES):
            cols = slice(p * LANES, (p + 1) * LANES)
            v2 = v[rows, cols]
            sv = bias_ref[:, cols]
            for q in range(per_tile):
                in_group = (lane >= q * C_GROUP_DIM) & (lane < (q + 1) * C_GROUP_DIM)
                vq = jnp.where(in_group, v2, 0.0).astype(BF16)
                sv = sv + jnp.where(in_group, _dot(w[p * per_tile + q], vq), 0.0)
            out_cols.append((u[rows, cols] * sv).astype(BF16))
        out_rows.append(jnp.concatenate(out_cols, axis=1))
    return jnp.concatenate(out_rows, axis=0)


def _emit_norm(z, g_ref, b_ref, out_refs):
    y = _layer_norm(z, g_ref[...], b_ref[...])
    for o_ref in out_refs:
        o_ref[...] = y.astype(o_ref.dtype)


def _merge_kernel(oa_ref, ob_ref, u_ref, v_ref, ga_ref, gb_ref, gc_ref, x_ref,
                  cg_ref, cb_ref, cw_ref, cbias_ref,
                  wa_ref, wb_ref, wc_ref, wo_ref, g_ref, b_ref, *out_refs, alpha):
    def branch(y, w_ref, gate_ref):
        gate = 0.5 * jnp.tanh(0.5 * gate_ref[...].astype(F32)) + 0.5
        return gate * _dot(y, w_ref[...])

    yc = _spatial_gate(u_ref[...].astype(F32), v_ref[...].astype(F32), cg_ref, cb_ref, cw_ref, cbias_ref)
    merged = (branch(oa_ref[...], wa_ref, ga_ref) + branch(ob_ref[...], wb_ref, gb_ref)
              + branch(yc, wc_ref, gc_ref))
    mix = _dot(merged.astype(BF16), wo_ref[...])
    _emit_norm(alpha * x_ref[...] + mix, g_ref, b_ref, out_refs)


def _merge(oa, ob, proj2d, u_col, v_col, gate_col, x, c_ln_g, c_ln_b, c_ws, c_bs,
           layer, wa, wb, wc, wo, ln_g, ln_b, alpha, tm):
    t, d = x.shape
    row = lambda i: (i, 0)
    const = lambda i: (0, 0)
    of_layer = lambda i: (layer, 0, 0)
    c_bias = jnp.repeat(c_bs.T, C_GROUP_DIM, axis=1)
    branch_spec = pl.BlockSpec((tm, ATT_WIDTH), row)
    uv_specs = [pl.BlockSpec((tm, C_WIDTH), lambda i, c=c: (i, c)) for c in (u_col, v_col)]
    gate_specs = [pl.BlockSpec((tm, d), lambda i, c=gate_col + n: (i, c)) for n in range(3)]
    c_specs = [pl.BlockSpec((1, C_WIDTH), const), pl.BlockSpec((1, C_WIDTH), const),
               pl.BlockSpec((C_GROUPS, C_CHUNK, C_CHUNK), lambda i: (0, 0, 0)),
               pl.BlockSpec((C_CHUNK, C_WIDTH), const)]
    w_br_spec = _resident_spec((None, ATT_WIDTH, d), of_layer)
    vec_spec = pl.BlockSpec((1, d), const)
    return pl.pallas_call(
        functools.partial(_merge_kernel, alpha=alpha),
        grid=(t // tm,),
        in_specs=[branch_spec, branch_spec, *uv_specs, *gate_specs, pl.BlockSpec((tm, d), row), *c_specs,
                  w_br_spec, w_br_spec, w_br_spec, _resident_spec((None, d, d), of_layer), vec_spec, vec_spec],
        out_specs=[pl.BlockSpec((tm, d), row), pl.BlockSpec((tm, d), row)],
        out_shape=[jax.ShapeDtypeStruct((t, d), F32), jax.ShapeDtypeStruct((t, d), BF16)],
        compiler_params=_params("parallel"),
        name="merge",
    )(oa, ob, proj2d, proj2d, proj2d, proj2d, proj2d, x,
      c_ln_g.reshape(1, -1), c_ln_b.reshape(1, -1), c_ws, c_bias,
      wa, wb, wc, wo, ln_g.reshape(1, -1), ln_b.reshape(1, -1))


def _cross_kernel(x_ref, xb_ref, k_ref, v_ref, wq_ref, wo_ref, g_ref, b_ref, *out_refs, alpha):
    q = (_dot(xb_ref[...], wq_ref[...]) * (X_HEAD_DIM ** -0.5)).astype(BF16)
    cols = [slice(h * X_HEAD_DIM, (h + 1) * X_HEAD_DIM) for h in range(X_HEADS)]
    scores = [_dot_nt(q[:, c], k_ref[:, c]) for c in cols]
    probs = [jnp.exp(s - jnp.max(s, axis=-1, keepdims=True)) for s in scores]
    heads = [(_dot(p.astype(BF16), v_ref[:, c]) / jnp.sum(p, axis=-1, keepdims=True)).astype(BF16)
             for p, c in zip(probs, cols)]
    cross = _dot(jnp.concatenate(heads, axis=-1), wo_ref[...])
    _emit_norm(alpha * x_ref[...] + cross, g_ref, b_ref, out_refs)


def _cross(x, xb, kv, layer, wq, wo, ln_g, ln_b, alpha, batch, tm):
    t, d = x.shape
    s = t // batch
    m = kv.shape[1]
    x3, xb3 = x.reshape(batch, s, d), xb.reshape(batch, s, d)
    row = lambda bi, i: (bi, i, 0)
    const = lambda bi, i: (0, 0)
    of_layer = lambda bi, i: (layer, 0, 0)
    outs = pl.pallas_call(
        functools.partial(_cross_kernel, alpha=alpha),
        grid=(batch, s // tm),
        in_specs=[pl.BlockSpec((None, tm, d), row), pl.BlockSpec((None, tm, d), row),
                  pl.BlockSpec((None, m, X_WIDTH), lambda bi, i: (bi, 0, 2 * layer)),
                  pl.BlockSpec((None, m, X_WIDTH), lambda bi, i: (bi, 0, 2 * layer + 1)),
                  pl.BlockSpec((None, d, X_WIDTH), of_layer), pl.BlockSpec((None, X_WIDTH, d), of_layer),
                  pl.BlockSpec((1, d), const), pl.BlockSpec((1, d), const)],
        out_specs=[pl.BlockSpec((None, tm, d), row), pl.BlockSpec((None, tm, d), row)],
        out_shape=[jax.ShapeDtypeStruct((batch, s, d), F32), jax.ShapeDtypeStruct((batch, s, d), BF16)],
        compiler_params=_params("parallel", "parallel"),
        name="cross",
    )(x3, xb3, kv, kv, wq, wo, ln_g.reshape(1, -1), ln_b.reshape(1, -1))
    return outs[0].reshape(t, d), outs[1].reshape(t, d)


def _ffn_kernel(x_ref, xb_ref, w1_ref, w2_ref, g_ref, b_ref, *out_refs, alpha):
    acc_ref = out_refs[0]
    j = pl.program_id(1)

    @pl.when(j == 0)
    def _():
        acc_ref[...] = jnp.zeros_like(acc_ref)

    h = jnp.maximum(_dot(xb_ref[...], w1_ref[...]), 0.0)
    acc_ref[...] += _dot((h * h).astype(BF16), w2_ref[...])

    @pl.when(j == pl.num_programs(1) - 1)
    def _():
        _emit_norm(alpha * x_ref[...] + acc_ref[...], g_ref, b_ref, out_refs)


def _ffn(x, xb, w1, w2, ln_g, ln_b, alpha, tm, tf, with_bf16):
    t, d = x.shape
    d_ff = w1.shape[1]
    row = lambda i, j: (i, 0)
    const = lambda i, j: (0, 0)
    out_dtypes = [F32, BF16] if with_bf16 else [F32]
    return pl.pallas_call(
        functools.partial(_ffn_kernel, alpha=alpha),
        grid=(t // tm, d_ff // tf),
        in_specs=[pl.BlockSpec((tm, d), row), pl.BlockSpec((tm, d), row),
                  pl.BlockSpec((d, tf), lambda i, j: (0, j)), pl.BlockSpec((tf, d), lambda i, j: (j, 0)),
                  pl.BlockSpec((1, d), const), pl.BlockSpec((1, d), const)],
        out_specs=[pl.BlockSpec((tm, d), row) for _ in out_dtypes],
        out_shape=[jax.ShapeDtypeStruct((t, d), dt) for dt in out_dtypes],
        compiler_params=_params("parallel", "arbitrary"),
        name="ffn",
    )(x, xb, w1, w2, ln_g.reshape(1, -1), ln_b.reshape(1, -1))


def kernel(x, mem, w_in, w_br_a, w_br_b, w_br_c, w_out, c_ln_g, c_ln_b, c_ws, c_bs,
           ln1_g, ln1_b, w_xq, w_xk, w_xv, w_xo, ln2_g, ln2_b, w_ff1, w_ff2, ln3_g, ln3_b):
    batch, seq, d = x.shape
    depth = w_in.shape[0]
    t = batch * seq
    alpha = (2 * depth) ** 0.25
    assert all(seq % window == 0 for window, _ in A_PATTERNS) and seq % ATT_BLOCK == 0
    assert w_in.shape[2] == 3 * ATT_WIDTH * 2 + 2 * C_WIDTH + 3 * d

    qa_blk, qb_blk = 0, 3 * ATT_WIDTH // LANES
    u_blk, v_blk = 6, 7
    gate_blk = (6 * ATT_WIDTH + 2 * C_WIDTH) // d

    xf = x.reshape(t, d)
    xb = xf
    wa, wb, wc, wo, wq, wxo = (w.astype(BF16) for w in (w_br_a, w_br_b, w_br_c, w_out, w_xq, w_xo))
    w_kv = jnp.concatenate([w for l in range(depth) for w in (w_xk[l], w_xv[l])], axis=1).astype(BF16)
    kv = _matmul(mem.reshape(-1, d).astype(BF16), w_kv, BF16, *TILES["mem_proj"], "mem_proj")
    kv = kv.reshape(batch, -1, depth * 2 * X_WIDTH)
    for l in range(depth):
        proj2d = _matmul_f32w(xb, w_in, l, BF16, *TILES["in_proj"], "in_proj")
        proj = proj2d.reshape(batch, seq, -1)
        oa, w1 = _attention(proj, qa_blk, "dilated", w_ff1, l, "dilated_attn")
        ob, w2 = _attention(proj, qb_blk, "moba", w_ff2, l, "moba_attn")
        xf, xb = _merge(oa, ob, proj2d, u_blk, v_blk, gate_blk, xf,
                        c_ln_g[l], c_ln_b[l], c_ws[l], c_bs[l],
                        l, wa, wb, wc, wo, ln1_g[l], ln1_b[l], alpha, TILES["merge"])
        xf, xb = _cross(xf, xb, kv, l, wq, wxo, ln2_g[l], ln2_b[l], alpha, batch, TILES["cross"])
        last = l == depth - 1
        outs = _ffn(xf, xb, w1, w2,
                    ln3_g[l], ln3_b[l], alpha, *TILES["ffn"], not last)
        xf = outs[0]
        xb = None if last else outs[1]
    return xf.reshape(batch, seq, d)
```

```python
import functools

import jax
import jax.numpy as jnp
from jax import lax
from jax.experimental import pallas as pl
from jax.experimental.pallas import tpu as pltpu

F32 = jnp.float32
BF16 = jnp.bfloat16

HEAD_DIM = 64
A_PATTERNS = ((128, 1), (512, 4), (2048, 16))
ATT_WIDTH = 512
MOBA_BLOCK = 256
MOBA_TOPK = 3
C_GROUPS = 8
C_GROUP_DIM = 64
C_WIDTH = C_GROUPS * C_GROUP_DIM
C_CHUNK = 128
X_HEADS = 4
X_HEAD_DIM = 128
X_WIDTH = X_HEADS * X_HEAD_DIM
LN_EPS = 1e-5
LOG2_E = 1.4426950408889634

LANES = 128
ATT_BLOCK = 256
ATT_HEADS_PER_STEP = 8
ATT_QUERY_BLOCKS_PER_STEP = 4
BF16_SUBLANES = 16
ONES_ROWS = BF16_SUBLANES
VMEM_LIMIT = 56 * 1024 * 1024

TILES = {"in_proj": (1024, 1024), "mem_proj": (1024, 1024), "merge": 256, "cross": 512, "ffn": (512, 1024)}


def _params(*sem):
    return pltpu.CompilerParams(dimension_semantics=sem, vmem_limit_bytes=VMEM_LIMIT)


def _resident_spec(shape, index_map):
    return pl.BlockSpec(shape, index_map, pipeline_mode=pl.Buffered(1))


def _layer_norm(z, g, b):
    mu = jnp.mean(z, axis=-1, keepdims=True)
    zc = z - mu
    var = jnp.mean(zc * zc, axis=-1, keepdims=True)
    return zc * lax.rsqrt(var + LN_EPS) * g + b


def _gelu_tanh(x):
    c = (2.0 / jnp.pi) ** 0.5
    return 0.5 * x * (1.0 + jnp.tanh(c * (x + 0.044715 * (x * x * x))))


def _dot(a, b):
    return jnp.dot(a, b, preferred_element_type=F32)


def _dot_nt(a, b):
    return lax.dot_general(a, b, (((1,), (1,)), ((), ())), preferred_element_type=F32)


def _matmul_kernel(a_ref, w_ref, o_ref):
    o_ref[...] = _dot(a_ref[...], w_ref[...]).astype(o_ref.dtype)


def _matmul(a, w, out_dtype, tm, tn, name):
    m, k = a.shape
    n = w.shape[1]
    tm, tn = min(tm, m), min(tn, n)
    assert m % tm == 0 and n % tn == 0
    return pl.pallas_call(
        _matmul_kernel,
        grid=(m // tm, n // tn),
        in_specs=[pl.BlockSpec((tm, k), lambda i, j: (i, 0)),
                  pl.BlockSpec((k, tn), lambda i, j: (0, j))],
        out_specs=pl.BlockSpec((tm, tn), lambda i, j: (i, j)),
        out_shape=jax.ShapeDtypeStruct((m, n), out_dtype),
        compiler_params=_params("parallel", "parallel"),
        name=name,
    )(a, w)


def _matmul_f32w_kernel(a_ref, w_ref, o_ref, wb_ref):
    @pl.when(pl.program_id(1) == 0)
    def _():
        wb_ref[...] = w_ref[...].astype(BF16)

    o_ref[...] = _dot(a_ref[...].astype(BF16), wb_ref[...]).astype(o_ref.dtype)


def _matmul_f32w(a, w_stack, layer, out_dtype, tm, tn, name):
    m, k = a.shape
    n = w_stack.shape[2]
    assert m % tm == 0 and n % tn == 0
    return pl.pallas_call(
        _matmul_f32w_kernel,
        grid=(n // tn, m // tm),
        in_specs=[pl.BlockSpec((tm, k), lambda j, i: (i, 0)),
                  pl.BlockSpec((None, k, tn), lambda j, i: (layer, 0, j))],
        out_specs=pl.BlockSpec((tm, tn), lambda j, i: (i, j)),
        out_shape=jax.ShapeDtypeStruct((m, n), out_dtype),
        scratch_shapes=[pltpu.VMEM((k, tn), BF16)],
        compiler_params=_params("parallel", "arbitrary"),
        name=name,
    )(a, w_stack)


def _moba_selection(kmean, q2, head, n_past):
    klane = lax.broadcasted_iota(jnp.int32, kmean.shape, 1)
    km = jnp.where((klane < HEAD_DIM) == (head == 0), kmean, 0.0)
    km_hi = km.astype(BF16)
    km_lo = (km - km_hi.astype(F32)).astype(BF16)
    gate = _dot_nt(km_hi, q2) + _dot_nt(km_lo, q2)
    row = lax.broadcasted_iota(jnp.int32, gate.shape, 0)
    gate = jnp.where(row < n_past, gate, -jnp.inf)
    sel = []
    for n in range(n_past):
        gn = gate[n:n + 1, :]
        ahead = jnp.where(gate > gn, 1.0, 0.0) + jnp.where((gate == gn) & (row < n), 1.0, 0.0)
        sel.append(jnp.where(jnp.sum(ahead, axis=0, keepdims=True) < MOBA_TOPK, 1.0, 0.0))
    return sel


def _attn_kernel(*refs, mode, n_blocks):
    if mode == "dilated":
        q_ref, k_ref, v_ref, w_ref, bias_ref, o_ref, wb_ref, vt_ref = refs
    else:
        q_ref, k_ref, v_ref, w_ref, o_ref, wb_ref, vt_ref, kmean_ref = refs
    wb_ref[...] = w_ref[...].astype(BF16)
    tb = ATT_BLOCK
    n_slots = q_ref.shape[0]
    n_steps = n_blocks // n_slots
    n_heads = q_ref.shape[2] // HEAD_DIM
    i = pl.program_id(2)
    lane = lax.broadcasted_iota(jnp.int32, (tb, LANES), 1)
    tile = lambda h: slice(h // 2 * LANES, (h // 2 + 1) * LANES)
    q2, qm = [], []
    for slot in range(n_slots):
        q2.append([])
        qm.append([])
        for t in range(n_heads // 2):
            qt = q_ref[slot, :, t * LANES:(t + 1) * LANES]
            q2[slot].append(qt * jnp.asarray(HEAD_DIM ** -0.5, BF16))
            ql = (qt.astype(F32) * (HEAD_DIM ** -0.5 * LOG2_E)).astype(BF16)
            zero = jnp.zeros_like(ql)
            qm[slot] += [jnp.where(lane < HEAD_DIM, ql, zero), jnp.where(lane >= HEAD_DIM, ql, zero)]

    @pl.when(i == 0)
    def _():
        for t in range(n_heads // 2):
            cols = slice(t * LANES, (t + 1) * LANES)
            for n in range(n_blocks):
                rows = slice(n * tb, (n + 1) * tb)
                vt_ref[cols, rows] = v_ref[rows, cols].astype(F32).T.astype(BF16)
        if mode == "moba":
            kmean_ref[...] = jnp.zeros_like(kmean_ref)
            for n in range(n_blocks):
                kb = k_ref[n * tb:(n + 1) * tb, :].astype(F32)
                kmean_ref[n:n + 1, :] = jnp.sum(kb, axis=0, keepdims=True) / tb

    if mode == "moba":
        key_pos = lax.broadcasted_iota(jnp.int32, (tb, tb), 0)
        qry_pos = lax.broadcasted_iota(jnp.int32, (tb, tb), 1)

    ones_rows = jnp.ones((ONES_ROWS, tb), BF16)
    key_rows = lambda j: slice(j * tb, (j + 1) * tb)

    def step(step_index):
        items = [(slot, step_index + slot * n_steps, h) for slot in range(n_slots) for h in range(n_heads)]
        sel = {}
        if mode == "moba":
            for slot, ib, h in items:
                if ib > MOBA_TOPK:
                    sel[slot, h] = _moba_selection(kmean_ref[:, tile(h)], q2[slot][h // 2], h % 2, ib)

        def score(item, j):
            slot, _, h = item
            return _dot_nt(k_ref[key_rows(j), tile(h)], qm[slot][h])

        def piece(item, j, s):
            slot, ib, h = item
            if mode == "dilated":
                off = (n_blocks - 1 - ib + j) * tb
                s = s + bias_ref[off:off + tb, :]
                m = jnp.max(s, axis=0, keepdims=True)
                return m, jnp.exp2(s - jnp.where(m == -jnp.inf, 0.0, m)).astype(BF16)
            if j == ib:
                s = jnp.where(key_pos <= qry_pos, s, -jnp.inf)
            m = jnp.max(s, axis=0, keepdims=True)
            p = jnp.exp2(s - m).astype(BF16)
            if j < ib and (slot, h) in sel:
                m = jnp.where(sel[slot, h][j] > 0.0, m, -jnp.inf)
            return m, p

        def weighted(item, j, p):
            h = item[2]
            vext = jnp.concatenate([vt_ref[h * HEAD_DIM:(h + 1) * HEAD_DIM, key_rows(j)], ones_rows], axis=0)
            return _dot(vext, p)

        n_keys = [ib + 1 for _, ib, _ in items]
        n_items = len(items)
        scores = [[score(items[0], j) for j in range(n_keys[0])]] + [[] for _ in range(n_items - 1)]
        pieces = [[] for _ in range(n_items)]
        pvs = [[] for _ in range(n_items)]
        for k in range(n_items + 1):
            for j in range(max(n_keys[max(k - 1, 0):k + 2])):
                if k + 1 < n_items and j < n_keys[k + 1]:
                    scores[k + 1].append(score(items[k + 1], j))
                if k >= 1 and j < n_keys[k - 1]:
                    pvs[k - 1].append(weighted(items[k - 1], j, pieces[k - 1][j][1]))
                if k < n_items and j < n_keys[k]:
                    pieces[k].append(piece(items[k], j, scores[k][j]))

        for slot in range(n_slots):
            outs = []
            for k in range(slot * n_heads, (slot + 1) * n_heads):
                m_all = functools.reduce(jnp.maximum, [m for m, _ in pieces[k]])
                acc = sum(jnp.exp2(m - m_all) * pv for (m, _), pv in zip(pieces[k], pvs[k]))
                outs.append(acc[:HEAD_DIM, :] / acc[HEAD_DIM:HEAD_DIM + 1, :])
            o_ref[slot] = jnp.concatenate(outs, axis=0).T.astype(o_ref.dtype)

    for step_index in range(n_steps):
        pl.when(i == step_index)(functools.partial(step, step_index))


def _dilated_bias(n_blocks):
    tb = ATT_BLOCK
    shape = (n_blocks, tb, tb)
    d = n_blocks - 1 - lax.broadcasted_iota(jnp.int32, shape, 0)
    c = lax.broadcasted_iota(jnp.int32, shape, 1)
    r = lax.broadcasted_iota(jnp.int32, shape, 2)
    delta = d * tb + r - c
    mult = jnp.zeros(shape, F32)
    for window, dilation in A_PATTERNS:
        hit = (delta >= 0) & (delta <= window) & (delta % dilation == 0)
        mult = mult + hit.astype(F32)
    return jnp.log2(mult).reshape(n_blocks * tb, tb)


def _attention(proj, q_col, mode, w_stack, layer, name):
    b, s, _ = proj.shape
    tb = ATT_BLOCK
    n_blocks = s // tb
    gw = ATT_HEADS_PER_STEP * HEAD_DIM
    n_groups = ATT_WIDTH // gw
    q_blk = q_col * LANES // gw
    k_blk, v_blk = q_blk + n_groups, q_blk + 2 * n_groups
    n_slots = ATT_QUERY_BLOCKS_PER_STEP
    n_steps = n_blocks // n_slots
    assert n_steps * n_slots == n_blocks
    proj_q = proj.reshape(b, n_slots, n_steps, tb, proj.shape[2])
    w_rows, w_cols = w_stack.shape[1:]
    n_slabs = b * n_groups * n_steps
    slab = w_rows // n_slabs
    assert slab * n_slabs == w_rows and slab % BF16_SUBLANES == 0
    slab_index = lambda bi, g, i: (bi * n_groups + g) * n_steps + i
    in_specs = [
        pl.BlockSpec((None, n_slots, None, tb, gw), lambda bi, g, i: (bi, 0, i, 0, q_blk + g)),
        pl.BlockSpec((None, s, gw), lambda bi, g, i: (bi, 0, k_blk + g)),
        pl.BlockSpec((None, s, gw), lambda bi, g, i: (bi, 0, v_blk + g)),
        pl.BlockSpec((None, slab, w_cols), lambda bi, g, i: (layer, slab_index(bi, g, i), 0)),
    ]
    args = [proj_q, proj, proj, w_stack]
    scratch = [pltpu.VMEM((gw, s), BF16)]
    if mode == "dilated":
        in_specs += [_resident_spec((s, tb), lambda bi, g, i: (0, 0))]
        args += [_dilated_bias(n_blocks)]
    else:
        scratch += [pltpu.VMEM((ONES_ROWS, gw), F32)]
    out, w_bf16 = pl.pallas_call(
        functools.partial(_attn_kernel, mode=mode, n_blocks=n_blocks),
        grid=(b, n_groups, n_steps),
        in_specs=in_specs,
        out_specs=[pl.BlockSpec((None, n_slots, None, tb, gw), lambda bi, g, i: (bi, 0, i, 0, g)),
                   pl.BlockSpec((slab, w_cols), lambda bi, g, i: (slab_index(bi, g, i), 0))],
        out_shape=[jax.ShapeDtypeStruct((b, n_slots, n_steps, tb, ATT_WIDTH), BF16),
                   jax.ShapeDtypeStruct((w_rows, w_cols), BF16)],
        scratch_shapes=scratch,
        compiler_params=_params("arbitrary", "arbitrary", "arbitrary"),
        name=name,
    )(*args)
    return out.reshape(b * s, ATT_WIDTH), w_bf16


def _spatial_gate(u, v, g_ref, b_ref, w_ref, bias_ref):
    u = _gelu_tanh(u)
    v = _layer_norm(_gelu_tanh(v), g_ref[...], b_ref[...])
    lane = lax.broadcasted_iota(jnp.int32, (C_CHUNK, LANES), 1)
    t_idx = lax.broadcasted_iota(jnp.int32, (C_CHUNK, C_CHUNK), 0)
    s_idx = lax.broadcasted_iota(jnp.int32, (C_CHUNK, C_CHUNK), 1)
    w = [jnp.where(s_idx <= t_idx, w_ref[g], 0.0).astype(BF16) for g in range(C_GROUPS)]
    per_tile = LANES // C_GROUP_DIM
    out_rows = []
    for ch in range(u.shape[0] // C_CHUNK):
        rows = slice(ch * C_CHUNK, (ch + 1) * C_CHUNK)
        out_cols = []
        for p in range(C_WIDTH // LANES):
            cols = slice(p * LANES, (p + 1) * LANES)
            v2 = v[rows, cols]
            sv = bias_ref[:, cols]
            for q in range(per_tile):
                in_group = (lane >= q * C_GROUP_DIM) & (lane < (q + 1) * C_GROUP_DIM)
                vq = jnp.where(in_group, v2, 0.0).astype(BF16)
                sv = sv + jnp.where(in_group, _dot(w[p * per_tile + q], vq), 0.0)
            out_cols.append((u[rows, cols] * sv).astype(BF16))
        out_rows.append(jnp.concatenate(out_cols, axis=1))
    return jnp.concatenate(out_rows, axis=0)


def _emit_norm(z, g_ref, b_ref, out_refs):
    y = _layer_norm(z, g_ref[...], b_ref[...])
    for o_ref in out_refs:
        o_ref[...] = y.astype(o_ref.dtype)


def _merge_kernel(oa_ref, ob_ref, u_ref, v_ref, ga_ref, gb_ref, gc_ref, x_ref,
                  cg_ref, cb_ref, cw_ref, cbias_ref,
                  wa_ref, wb_ref, wc_ref, wo_ref, g_ref, b_ref, *out_refs, alpha):
    def branch(y, w_ref, gate_ref):
        gate = 0.5 * jnp.tanh(0.5 * gate_ref[...].astype(F32)) + 0.5
        return gate * _dot(y, w_ref[...])

    yc = _spatial_gate(u_ref[...].astype(F32), v_ref[...].astype(F32), cg_ref, cb_ref, cw_ref, cbias_ref)
    merged = (branch(oa_ref[...], wa_ref, ga_ref) + branch(ob_ref[...], wb_ref, gb_ref)
              + branch(yc, wc_ref, gc_ref))
    mix = _dot(merged.astype(BF16), wo_ref[...])
    _emit_norm(alpha * x_ref[...] + mix, g_ref, b_ref, out_refs)


def _merge(oa, ob, proj2d, u_col, v_col, gate_col, x, c_ln_g, c_ln_b, c_ws, c_bs,
           layer, wa, wb, wc, wo, ln_g, ln_b, alpha, tm):
    t, d = x.shape
    row = lambda i: (i, 0)
    const = lambda i: (0, 0)
    of_layer = lambda i: (layer, 0, 0)
    c_bias = jnp.repeat(c_bs.T, C_GROUP_DIM, axis=1)
    branch_spec = pl.BlockSpec((tm, ATT_WIDTH), row)
    uv_specs = [pl.BlockSpec((tm, C_WIDTH), lambda i, c=c: (i, c)) for c in (u_col, v_col)]
    gate_specs = [pl.BlockSpec((tm, d), lambda i, c=gate_col + n: (i, c)) for n in range(3)]
    c_specs = [pl.BlockSpec((1, C_WIDTH), const), pl.BlockSpec((1, C_WIDTH), const),
               pl.BlockSpec((C_GROUPS, C_CHUNK, C_CHUNK), lambda i: (0, 0, 0)),
               pl.BlockSpec((C_CHUNK, C_WIDTH), const)]
    w_br_spec = _resident_spec((None, ATT_WIDTH, d), of_layer)
    vec_spec = pl.BlockSpec((1, d), const)
    return pl.pallas_call(
        functools.partial(_merge_kernel, alpha=alpha),
        grid=(t // tm,),
        in_specs=[branch_spec, branch_spec, *uv_specs, *gate_specs, pl.BlockSpec((tm, d), row), *c_specs,
                  w_br_spec, w_br_spec, w_br_spec, _resident_spec((None, d, d), of_layer), vec_spec, vec_spec],
        out_specs=[pl.BlockSpec((tm, d), row), pl.BlockSpec((tm, d), row)],
        out_shape=[jax.ShapeDtypeStruct((t, d), F32), jax.ShapeDtypeStruct((t, d), BF16)],
        compiler_params=_params("parallel"),
        name="merge",
    )(oa, ob, proj2d, proj2d, proj2d, proj2d, proj2d, x,
      c_ln_g.reshape(1, -1), c_ln_b.reshape(1, -1), c_ws, c_bias,
      wa, wb, wc, wo, ln_g.reshape(1, -1), ln_b.reshape(1, -1))


def _cross_kernel(x_ref, xb_ref, k_ref, v_ref, wq_ref, wo_ref, g_ref, b_ref, *out_refs, alpha):
    cols = [slice(h * X_HEAD_DIM, (h + 1) * X_HEAD_DIM) for h in range(X_HEADS)]
    half = x_ref.shape[0] // 2
    row_halves = (slice(0, half), slice(half, 2 * half))

    def project(rows):
        return (_dot(xb_ref[rows, :], wq_ref[...]) * (X_HEAD_DIM ** -0.5)).astype(BF16)

    def scores(q):
        return [_dot_nt(q[:, c], k_ref[:, c]) for c in cols]

    def softmax(ss):
        return [jnp.exp(s - jnp.max(s, axis=-1, keepdims=True)) for s in ss]

    def attend(ps):
        heads = [(_dot(p.astype(BF16), v_ref[:, c]) / jnp.sum(p, axis=-1, keepdims=True)).astype(BF16)
                 for p, c in zip(ps, cols)]
        return jnp.concatenate(heads, axis=-1)

    def residual(o, rows):
        return alpha * x_ref[rows, :] + _dot(o, wo_ref[...])

    def norm(z, rows):
        y = _layer_norm(z, g_ref[...], b_ref[...])
        for o_ref in out_refs:
            o_ref[rows, :] = y.astype(o_ref.dtype)

    a, b = row_halves
    q_a = project(a)
    q_b = project(b)
    s_a = scores(q_a)
    p_a = softmax(s_a)
    s_b = scores(q_b)
    o_a = attend(p_a)
    p_b = softmax(s_b)
    z_a = residual(o_a, a)
    o_b = attend(p_b)
    norm(z_a, a)
    z_b = residual(o_b, b)
    norm(z_b, b)


def _cross(x, xb, kv, layer, wq, wo, ln_g, ln_b, alpha, batch, tm):
    t, d = x.shape
    s = t // batch
    m = kv.shape[1]
    x3, xb3 = x.reshape(batch, s, d), xb.reshape(batch, s, d)
    row = lambda bi, i: (bi, i, 0)
    const = lambda bi, i: (0, 0)
    of_layer = lambda bi, i: (layer, 0, 0)
    outs = pl.pallas_call(
        functools.partial(_cross_kernel, alpha=alpha),
        grid=(batch, s // tm),
        in_specs=[pl.BlockSpec((None, tm, d), row), pl.BlockSpec((None, tm, d), row),
                  pl.BlockSpec((None, m, X_WIDTH), lambda bi, i: (bi, 0, 2 * layer)),
                  pl.BlockSpec((None, m, X_WIDTH), lambda bi, i: (bi, 0, 2 * layer + 1)),
                  pl.BlockSpec((None, d, X_WIDTH), of_layer), pl.BlockSpec((None, X_WIDTH, d), of_layer),
                  pl.BlockSpec((1, d), const), pl.BlockSpec((1, d), const)],
        out_specs=[pl.BlockSpec((None, tm, d), row), pl.BlockSpec((None, tm, d), row)],
        out_shape=[jax.ShapeDtypeStruct((batch, s, d), F32), jax.ShapeDtypeStruct((batch, s, d), BF16)],
        compiler_params=_params("parallel", "parallel"),
        name="cross",
    )(x3, xb3, kv, kv, wq, wo, ln_g.reshape(1, -1), ln_b.reshape(1, -1))
    return outs[0].reshape(t, d), outs[1].reshape(t, d)


def _ffn_kernel(x_ref, xb_ref, w1_ref, w2_ref, g_ref, b_ref, *out_refs, alpha):
    acc_ref = out_refs[0]
    j = pl.program_id(1)

    @pl.when(j == 0)
    def _():
        acc_ref[...] = jnp.zeros_like(acc_ref)

    h = jnp.maximum(_dot(xb_ref[...], w1_ref[...]), 0.0)
    acc_ref[...] += _dot((h * h).astype(BF16), w2_ref[...])

    @pl.when(j == pl.num_programs(1) - 1)
    def _():
        _emit_norm(alpha * x_ref[...] + acc_ref[...], g_ref, b_ref, out_refs)


def _ffn(x, xb, w1, w2, ln_g, ln_b, alpha, tm, tf, with_bf16):
    t, d = x.shape
    d_ff = w1.shape[1]
    row = lambda i, j: (i, 0)
    const = lambda i, j: (0, 0)
    out_dtypes = [F32, BF16] if with_bf16 else [F32]
    return pl.pallas_call(
        functools.partial(_ffn_kernel, alpha=alpha),
        grid=(t // tm, d_ff // tf),
        in_specs=[pl.BlockSpec((tm, d), row), pl.BlockSpec((tm, d), row),
                  pl.BlockSpec((d, tf), lambda i, j: (0, j)), pl.BlockSpec((tf, d), lambda i, j: (j, 0)),
                  pl.BlockSpec((1, d), const), pl.BlockSpec((1, d), const)],
        out_specs=[pl.BlockSpec((tm, d), row) for _ in out_dtypes],
        out_shape=[jax.ShapeDtypeStruct((t, d), dt) for dt in out_dtypes],
        compiler_params=_params("parallel", "arbitrary"),
        name="ffn",
    )(x, xb, w1, w2, ln_g.reshape(1, -1), ln_b.reshape(1, -1))


def kernel(x, mem, w_in, w_br_a, w_br_b, w_br_c, w_out, c_ln_g, c_ln_b, c_ws, c_bs,
           ln1_g, ln1_b, w_xq, w_xk, w_xv, w_xo, ln2_g, ln2_b, w_ff1, w_ff2, ln3_g, ln3_b):
    batch, seq, d = x.shape
    depth = w_in.shape[0]
    t = batch * seq
    alpha = (2 * depth) ** 0.25
    assert all(seq % window == 0 for window, _ in A_PATTERNS) and seq % ATT_BLOCK == 0
    assert w_in.shape[2] == 3 * ATT_WIDTH * 2 + 2 * C_WIDTH + 3 * d

    qa_blk, qb_blk = 0, 3 * ATT_WIDTH // LANES
    u_blk, v_blk = 6, 7
    gate_blk = (6 * ATT_WIDTH + 2 * C_WIDTH) // d

    xf = x.reshape(t, d)
    xb = xf
    wa, wb, wc, wo, wq, wxo = (w.astype(BF16) for w in (w_br_a, w_br_b, w_br_c, w_out, w_xq, w_xo))
    w_kv = jnp.concatenate([w for l in range(depth) for w in (w_xk[l], w_xv[l])], axis=1).astype(BF16)
    kv = _matmul(mem.reshape(-1, d).astype(BF16), w_kv, BF16, *TILES["mem_proj"], "mem_proj")
    kv = kv.reshape(batch, -1, depth * 2 * X_WIDTH)
    for l in range(depth):
        proj2d = _matmul_f32w(xb, w_in, l, BF16, *TILES["in_proj"], "in_proj")
        proj = proj2d.reshape(batch, seq, -1)
        oa, w1 = _attention(proj, qa_blk, "dilated", w_ff1, l, "dilated_attn")
        ob, w2 = _attention(proj, qb_blk, "moba", w_ff2, l, "moba_attn")
        xf, xb = _merge(oa, ob, proj2d, u_blk, v_blk, gate_blk, xf,
                        c_ln_g[l], c_ln_b[l], c_ws[l], c_bs[l],
                        l, wa, wb, wc, wo, ln1_g[l], ln1_b[l], alpha, TILES["merge"])
        xf, xb = _cross(xf, xb, kv, l, wq, wxo, ln2_g[l], ln2_b[l], alpha, batch, TILES["cross"])
        last = l == depth - 1
        outs = _ffn(xf, xb, w1, w2,
                    ln3_g[l], ln3_b[l], alpha, *TILES["ffn"], not last)
        xf = outs[0]
        xb = None if last else outs[1]
    return xf.reshape(batch, seq, d)
```

```python
import functools

import jax
import jax.numpy as jnp
from jax import lax
from jax.experimental import pallas as pl
from jax.experimental.pallas import tpu as pltpu

F32 = jnp.float32
BF16 = jnp.bfloat16

HEAD_DIM = 64
A_PATTERNS = ((128, 1), (512, 4), (2048, 16))
ATT_WIDTH = 512
MOBA_BLOCK = 256
MOBA_TOPK = 3
C_GROUPS = 8
C_GROUP_DIM = 64
C_WIDTH = C_GROUPS * C_GROUP_DIM
C_CHUNK = 128
X_HEADS = 4
X_HEAD_DIM = 128
X_WIDTH = X_HEADS * X_HEAD_DIM
LN_EPS = 1e-5
LOG2_E = 1.4426950408889634

LANES = 128
ATT_BLOCK = 256
ATT_HEADS_PER_STEP = 8
ATT_QUERY_BLOCKS_PER_STEP = 4
BF16_SUBLANES = 16
ONES_ROWS = BF16_SUBLANES
VMEM_LIMIT = 56 * 1024 * 1024

TILES = {"in_proj": (1024, 1024), "mem_proj": (1024, 1024), "merge": 256, "cross": 512, "ffn": (512, 1024)}


def _params(*sem):
    return pltpu.CompilerParams(dimension_semantics=sem, vmem_limit_bytes=VMEM_LIMIT)


def _resident_spec(shape, index_map):
    return pl.BlockSpec(shape, index_map, pipeline_mode=pl.Buffered(1))


def _layer_norm(z, g, b):
    mu = jnp.mean(z, axis=-1, keepdims=True)
    zc = z - mu
    var = jnp.mean(zc * zc, axis=-1, keepdims=True)
    return zc * lax.rsqrt(var + LN_EPS) * g + b


def _gelu_tanh(x):
    c = (2.0 / jnp.pi) ** 0.5
    return 0.5 * x * (1.0 + jnp.tanh(c * (x + 0.044715 * (x * x * x))))


def _dot(a, b):
    return jnp.dot(a, b, preferred_element_type=F32)


def _dot_nt(a, b):
    return lax.dot_general(a, b, (((1,), (1,)), ((), ())), preferred_element_type=F32)


def _matmul_kernel(a_ref, w_ref, o_ref):
    o_ref[...] = _dot(a_ref[...], w_ref[...]).astype(o_ref.dtype)


def _matmul(a, w, out_dtype, tm, tn, name):
    m, k = a.shape
    n = w.shape[1]
    tm, tn = min(tm, m), min(tn, n)
    assert m % tm == 0 and n % tn == 0
    return pl.pallas_call(
        _matmul_kernel,
        grid=(m // tm, n // tn),
        in_specs=[pl.BlockSpec((tm, k), lambda i, j: (i, 0)),
                  pl.BlockSpec((k, tn), lambda i, j: (0, j))],
        out_specs=pl.BlockSpec((tm, tn), lambda i, j: (i, j)),
        out_shape=jax.ShapeDtypeStruct((m, n), out_dtype),
        compiler_params=_params("parallel", "parallel"),
        name=name,
    )(a, w)


def _matmul_f32w_kernel(a_ref, w_ref, o_ref, wb_ref):
    @pl.when(pl.program_id(1) == 0)
    def _():
        wb_ref[...] = w_ref[...].astype(BF16)

    o_ref[...] = _dot(a_ref[...].astype(BF16), wb_ref[...]).astype(o_ref.dtype)


def _matmul_f32w(a, w_stack, layer, out_dtype, tm, tn, name):
    m, k = a.shape
    n = w_stack.shape[2]
    assert m % tm == 0 and n % tn == 0
    return pl.pallas_call(
        _matmul_f32w_kernel,
        grid=(n // tn, m // tm),
        in_specs=[pl.BlockSpec((tm, k), lambda j, i: (i, 0)),
                  pl.BlockSpec((None, k, tn), lambda j, i: (layer, 0, j))],
        out_specs=pl.BlockSpec((tm, tn), lambda j, i: (i, j)),
        out_shape=jax.ShapeDtypeStruct((m, n), out_dtype),
        scratch_shapes=[pltpu.VMEM((k, tn), BF16)],
        compiler_params=_params("parallel", "arbitrary"),
        name=name,
    )(a, w_stack)


def _moba_selection(kmean, q2, head, n_past):
    klane = lax.broadcasted_iota(jnp.int32, kmean.shape, 1)
    km = jnp.where((klane < HEAD_DIM) == (head == 0), kmean, 0.0)
    km_hi = km.astype(BF16)
    km_lo = (km - km_hi.astype(F32)).astype(BF16)
    gate = _dot_nt(km_hi, q2) + _dot_nt(km_lo, q2)
    row = lax.broadcasted_iota(jnp.int32, gate.shape, 0)
    gate = jnp.where(row < n_past, gate, -jnp.inf)
    sel = []
    for n in range(n_past):
        gn = gate[n:n + 1, :]
        ahead = jnp.where(gate > gn, 1.0, 0.0) + jnp.where((gate == gn) & (row < n), 1.0, 0.0)
        sel.append(jnp.where(jnp.sum(ahead, axis=0, keepdims=True) < MOBA_TOPK, 1.0, 0.0))
    return sel


def _attn_kernel(*refs, mode, n_blocks):
    if mode == "dilated":
        q_ref, k_ref, v_ref, w_ref, bias_ref, o_ref, wb_ref, vt_ref = refs
    else:
        q_ref, k_ref, v_ref, w_ref, o_ref, wb_ref, vt_ref, kmean_ref = refs
    tb = ATT_BLOCK
    n_slots = q_ref.shape[0]
    n_steps = n_blocks // n_slots
    n_heads = q_ref.shape[2] // HEAD_DIM
    i = pl.program_id(2)
    lane = lax.broadcasted_iota(jnp.int32, (tb, LANES), 1)
    tile = lambda h: slice(h // 2 * LANES, (h // 2 + 1) * LANES)
    q2, qm = [], []
    for slot in range(n_slots):
        q2.append([])
        qm.append([])
        for t in range(n_heads // 2):
            qt = q_ref[slot, :, t * LANES:(t + 1) * LANES]
            q2[slot].append(qt * jnp.asarray(HEAD_DIM ** -0.5, BF16))
            ql = (qt.astype(F32) * (HEAD_DIM ** -0.5 * LOG2_E)).astype(BF16)
            zero = jnp.zeros_like(ql)
            qm[slot] += [jnp.where(lane < HEAD_DIM, ql, zero), jnp.where(lane >= HEAD_DIM, ql, zero)]

    def setup():
        for t in range(n_heads // 2):
            cols = slice(t * LANES, (t + 1) * LANES)
            for n in range(n_blocks):
                rows = slice(n * tb, (n + 1) * tb)
                vt_ref[cols, rows] = v_ref[rows, cols].astype(F32).T.astype(BF16)
        if mode == "moba":
            kmean_ref[...] = jnp.zeros_like(kmean_ref)
            for n in range(n_blocks):
                kb = k_ref[n * tb:(n + 1) * tb, :].astype(F32)
                kmean_ref[n:n + 1, :] = jnp.sum(kb, axis=0, keepdims=True) / tb

    if mode == "moba":
        key_pos = lax.broadcasted_iota(jnp.int32, (tb, tb), 0)
        qry_pos = lax.broadcasted_iota(jnp.int32, (tb, tb), 1)

    ones_rows = jnp.ones((ONES_ROWS, tb), BF16)
    key_rows = lambda j: slice(j * tb, (j + 1) * tb)

    def step(step_index):
        wb_ref[...] = w_ref[...].astype(BF16)
        if step_index == 0:
            setup()
        items = [(slot, step_index + slot * n_steps, h) for slot in range(n_slots) for h in range(n_heads)]
        sel = {}
        if mode == "moba":
            for slot, ib, h in items:
                if ib > MOBA_TOPK:
                    sel[slot, h] = _moba_selection(kmean_ref[:, tile(h)], q2[slot][h // 2], h % 2, ib)

        def score(item, j):
            slot, _, h = item
            return _dot_nt(k_ref[key_rows(j), tile(h)], qm[slot][h])

        def piece(item, j, s):
            slot, ib, h = item
            if mode == "dilated":
                off = (n_blocks - 1 - ib + j) * tb
                s = s + bias_ref[off:off + tb, :]
                m = jnp.max(s, axis=0, keepdims=True)
                return m, jnp.exp2(s - jnp.where(m == -jnp.inf, 0.0, m)).astype(BF16)
            if j == ib:
                s = jnp.where(key_pos <= qry_pos, s, -jnp.inf)
            m = jnp.max(s, axis=0, keepdims=True)
            p = jnp.exp2(s - m).astype(BF16)
            if j < ib and (slot, h) in sel:
                m = jnp.where(sel[slot, h][j] > 0.0, m, -jnp.inf)
            return m, p

        def weighted(item, j, p):
            h = item[2]
            vext = jnp.concatenate([vt_ref[h * HEAD_DIM:(h + 1) * HEAD_DIM, key_rows(j)], ones_rows], axis=0)
            return _dot(vext, p)

        n_keys = [ib + 1 for _, ib, _ in items]
        n_items = len(items)
        scores = [[score(items[0], j) for j in range(n_keys[0])]] + [[] for _ in range(n_items - 1)]
        pieces = [[] for _ in range(n_items)]
        pvs = [[] for _ in range(n_items)]
        for k in range(n_items + 1):
            for j in range(max(n_keys[max(k - 1, 0):k + 2])):
                if k + 1 < n_items and j < n_keys[k + 1]:
                    scores[k + 1].append(score(items[k + 1], j))
                if k >= 1 and j < n_keys[k - 1]:
                    pvs[k - 1].append(weighted(items[k - 1], j, pieces[k - 1][j][1]))
                if k < n_items and j < n_keys[k]:
                    pieces[k].append(piece(items[k], j, scores[k][j]))

        for slot in range(n_slots):
            outs = []
            for k in range(slot * n_heads, (slot + 1) * n_heads):
                m_all = functools.reduce(jnp.maximum, [m for m, _ in pieces[k]])
                acc = sum(jnp.exp2(m - m_all) * pv for (m, _), pv in zip(pieces[k], pvs[k]))
                outs.append(acc[:HEAD_DIM, :] / acc[HEAD_DIM:HEAD_DIM + 1, :])
            o_ref[slot] = jnp.concatenate(outs, axis=0).T.astype(o_ref.dtype)

    for step_index in range(n_steps):
        pl.when(i == step_index)(functools.partial(step, step_index))


def _dilated_bias(n_blocks):
    tb = ATT_BLOCK
    shape = (n_blocks, tb, tb)
    d = n_blocks - 1 - lax.broadcasted_iota(jnp.int32, shape, 0)
    c = lax.broadcasted_iota(jnp.int32, shape, 1)
    r = lax.broadcasted_iota(jnp.int32, shape, 2)
    delta = d * tb + r - c
    mult = jnp.zeros(shape, F32)
    for window, dilation in A_PATTERNS:
        hit = (delta >= 0) & (delta <= window) & (delta % dilation == 0)
        mult = mult + hit.astype(F32)
    return jnp.log2(mult).reshape(n_blocks * tb, tb)


def _attention(proj, q_col, mode, w_stack, layer, name):
    b, s, _ = proj.shape
    tb = ATT_BLOCK
    n_blocks = s // tb
    gw = ATT_HEADS_PER_STEP * HEAD_DIM
    n_groups = ATT_WIDTH // gw
    q_blk = q_col * LANES // gw
    k_blk, v_blk = q_blk + n_groups, q_blk + 2 * n_groups
    n_slots = ATT_QUERY_BLOCKS_PER_STEP
    n_steps = n_blocks // n_slots
    assert n_steps * n_slots == n_blocks
    proj_q = proj.reshape(b, n_slots, n_steps, tb, proj.shape[2])
    w_rows, w_cols = w_stack.shape[1:]
    n_slabs = b * n_groups * n_steps
    slab = w_rows // n_slabs
    assert slab * n_slabs == w_rows and slab % BF16_SUBLANES == 0
    slab_index = lambda bi, g, i: (bi * n_groups + g) * n_steps + i
    in_specs = [
        pl.BlockSpec((None, n_slots, None, tb, gw), lambda bi, g, i: (bi, 0, i, 0, q_blk + g)),
        pl.BlockSpec((None, s, gw), lambda bi, g, i: (bi, 0, k_blk + g)),
        pl.BlockSpec((None, s, gw), lambda bi, g, i: (bi, 0, v_blk + g)),
        pl.BlockSpec((None, slab, w_cols), lambda bi, g, i: (layer, slab_index(bi, g, i), 0)),
    ]
    args = [proj_q, proj, proj, w_stack]
    scratch = [pltpu.VMEM((gw, s), BF16)]
    if mode == "dilated":
        in_specs += [_resident_spec((s, tb), lambda bi, g, i: (0, 0))]
        args += [_dilated_bias(n_blocks)]
    else:
        scratch += [pltpu.VMEM((ONES_ROWS, gw), F32)]
    out, w_bf16 = pl.pallas_call(
        functools.partial(_attn_kernel, mode=mode, n_blocks=n_blocks),
        grid=(b, n_groups, n_steps),
        in_specs=in_specs,
        out_specs=[pl.BlockSpec((None, n_slots, None, tb, gw), lambda bi, g, i: (bi, 0, i, 0, g)),
                   pl.BlockSpec((slab, w_cols), lambda bi, g, i: (slab_index(bi, g, i), 0))],
        out_shape=[jax.ShapeDtypeStruct((b, n_slots, n_steps, tb, ATT_WIDTH), BF16),
                   jax.ShapeDtypeStruct((w_rows, w_cols), BF16)],
        scratch_shapes=scratch,
        compiler_params=_params("arbitrary", "arbitrary", "arbitrary"),
        name=name,
    )(*args)
    return out.reshape(b * s, ATT_WIDTH), w_bf16


def _spatial_gate(u, v, g_ref, b_ref, w_ref, bias_ref):
    u = _gelu_tanh(u)
    v = _layer_norm(_gelu_tanh(v), g_ref[...], b_ref[...])
    lane = lax.broadcasted_iota(jnp.int32, (C_CHUNK, LANES), 1)
    t_idx = lax.broadcasted_iota(jnp.int32, (C_CHUNK, C_CHUNK), 0)
    s_idx = lax.broadcasted_iota(jnp.int32, (C_CHUNK, C_CHUNK), 1)
    w = [jnp.where(s_idx <= t_idx, w_ref[g], 0.0).astype(BF16) for g in range(C_GROUPS)]
    per_tile = LANES // C_GROUP_DIM
    out_rows = []
    for ch in range(u.shape[0] // C_CHUNK):
        rows = slice(ch * C_CHUNK, (ch + 1) * C_CHUNK)
        out_cols = []
        for p in range(C_WIDTH // LANES):
            cols = slice(p * LANES, (p + 1) * LANES)
            v2 = v[rows, cols]
            sv = bias_ref[:, cols]
            for q in range(per_tile):
                in_group = (lane >= q * C_GROUP_DIM) & (lane < (q + 1) * C_GROUP_DIM)
                vq = jnp.where(in_group, v2, 0.0).astype(BF16)
                sv = sv + jnp.where(in_group, _dot(w[p * per_tile + q], vq), 0.0)
            out_cols.append((u[rows, cols] * sv).astype(BF16))
        out_rows.append(jnp.concatenate(out_cols, axis=1))
    return jnp.concatenate(out_rows, axis=0)


def _emit_norm(z, g_ref, b_ref, out_refs):
    y = _layer_norm(z, g_ref[...], b_ref[...])
    for o_ref in out_refs:
        o_ref[...] = y.astype(o_ref.dtype)


def _merge_kernel(oa_ref, ob_ref, u_ref, v_ref, ga_ref, gb_ref, gc_ref, x_ref,
                  cg_ref, cb_ref, cw_ref, cbias_ref,
                  wa_ref, wb_ref, wc_ref, wo_ref, g_ref, b_ref, *out_refs, alpha):
    def branch(y, w_ref, gate_ref):
        gate = 0.5 * jnp.tanh(0.5 * gate_ref[...].astype(F32)) + 0.5
        return gate * _dot(y, w_ref[...])

    yc = _spatial_gate(u_ref[...].astype(F32), v_ref[...].astype(F32), cg_ref, cb_ref, cw_ref, cbias_ref)
    merged = (branch(oa_ref[...], wa_ref, ga_ref) + branch(ob_ref[...], wb_ref, gb_ref)
              + branch(yc, wc_ref, gc_ref))
    mix = _dot(merged.astype(BF16), wo_ref[...])
    _emit_norm(alpha * x_ref[...] + mix, g_ref, b_ref, out_refs)


def _merge(oa, ob, proj2d, u_col, v_col, gate_col, x, c_ln_g, c_ln_b, c_ws, c_bs,
           layer, wa, wb, wc, wo, ln_g, ln_b, alpha, tm):
    t, d = x.shape
    row = lambda i: (i, 0)
    const = lambda i: (0, 0)
    of_layer = lambda i: (layer, 0, 0)
    c_bias = jnp.repeat(c_bs.T, C_GROUP_DIM, axis=1)
    branch_spec = pl.BlockSpec((tm, ATT_WIDTH), row)
    uv_specs = [pl.BlockSpec((tm, C_WIDTH), lambda i, c=c: (i, c)) for c in (u_col, v_col)]
    gate_specs = [pl.BlockSpec((tm, d), lambda i, c=gate_col + n: (i, c)) for n in range(3)]
    c_specs = [pl.BlockSpec((1, C_WIDTH), const), pl.BlockSpec((1, C_WIDTH), const),
               pl.BlockSpec((C_GROUPS, C_CHUNK, C_CHUNK), lambda i: (0, 0, 0)),
               pl.BlockSpec((C_CHUNK, C_WIDTH), const)]
    w_br_spec = _resident_spec((None, ATT_WIDTH, d), of_layer)
    vec_spec = pl.BlockSpec((1, d), const)
    return pl.pallas_call(
        functools.partial(_merge_kernel, alpha=alpha),
        grid=(t // tm,),
        in_specs=[branch_spec, branch_spec, *uv_specs, *gate_specs, pl.BlockSpec((tm, d), row), *c_specs,
                  w_br_spec, w_br_spec, w_br_spec, _resident_spec((None, d, d), of_layer), vec_spec, vec_spec],
        out_specs=[pl.BlockSpec((tm, d), row), pl.BlockSpec((tm, d), row)],
        out_shape=[jax.ShapeDtypeStruct((t, d), F32), jax.ShapeDtypeStruct((t, d), BF16)],
        compiler_params=_params("parallel"),
        name="merge",
    )(oa, ob, proj2d, proj2d, proj2d, proj2d, proj2d, x,
      c_ln_g.reshape(1, -1), c_ln_b.reshape(1, -1), c_ws, c_bias,
      wa, wb, wc, wo, ln_g.reshape(1, -1), ln_b.reshape(1, -1))


def _cross_kernel(x_ref, xb_ref, k_ref, v_ref, wq_ref, wo_ref, g_ref, b_ref, *out_refs, alpha):
    q = (_dot(xb_ref[...], wq_ref[...]) * (X_HEAD_DIM ** -0.5)).astype(BF16)
    cols = [slice(h * X_HEAD_DIM, (h + 1) * X_HEAD_DIM) for h in range(X_HEADS)]
    scores = [_dot_nt(q[:, c], k_ref[:, c]) for c in cols]
    probs = [jnp.exp(s - jnp.max(s, axis=-1, keepdims=True)) for s in scores]
    heads = [(_dot(p.astype(BF16), v_ref[:, c]) / jnp.sum(p, axis=-1, keepdims=True)).astype(BF16)
             for p, c in zip(probs, cols)]
    cross = _dot(jnp.concatenate(heads, axis=-1), wo_ref[...])
    _emit_norm(alpha * x_ref[...] + cross, g_ref, b_ref, out_refs)


def _cross(x, xb, kv, layer, wq, wo, ln_g, ln_b, alpha, batch, tm):
    t, d = x.shape
    s = t // batch
    m = kv.shape[1]
    x3, xb3 = x.reshape(batch, s, d), xb.reshape(batch, s, d)
    row = lambda bi, i: (bi, i, 0)
    const = lambda bi, i: (0, 0)
    of_layer = lambda bi, i: (layer, 0, 0)
    outs = pl.pallas_call(
        functools.partial(_cross_kernel, alpha=alpha),
        grid=(batch, s // tm),
        in_specs=[pl.BlockSpec((None, tm, d), row), pl.BlockSpec((None, tm, d), row),
                  pl.BlockSpec((None, m, X_WIDTH), lambda bi, i: (bi, 0, 2 * layer)),
                  pl.BlockSpec((None, m, X_WIDTH), lambda bi, i: (bi, 0, 2 * layer + 1)),
                  pl.BlockSpec((None, d, X_WIDTH), of_layer), pl.BlockSpec((None, X_WIDTH, d), of_layer),
                  pl.BlockSpec((1, d), const), pl.BlockSpec((1, d), const)],
        out_specs=[pl.BlockSpec((None, tm, d), row), pl.BlockSpec((None, tm, d), row)],
        out_shape=[jax.ShapeDtypeStruct((batch, s, d), F32), jax.ShapeDtypeStruct((batch, s, d), BF16)],
        compiler_params=_params("parallel", "parallel"),
        name="cross",
    )(x3, xb3, kv, kv, wq, wo, ln_g.reshape(1, -1), ln_b.reshape(1, -1))
    return outs[0].reshape(t, d), outs[1].reshape(t, d)


def _ffn_kernel(x_ref, xb_ref, w1_ref, w2_ref, g_ref, b_ref, *out_refs, alpha):
    acc_ref = out_refs[0]
    j = pl.program_id(1)

    @pl.when(j == 0)
    def _():
        acc_ref[...] = jnp.zeros_like(acc_ref)

    h = jnp.maximum(_dot(xb_ref[...], w1_ref[...]), 0.0)
    acc_ref[...] += _dot((h * h).astype(BF16), w2_ref[...])

    @pl.when(j == pl.num_programs(1) - 1)
    def _():
        _emit_norm(alpha * x_ref[...] + acc_ref[...], g_ref, b_ref, out_refs)


def _ffn(x, xb, w1, w2, ln_g, ln_b, alpha, tm, tf, with_bf16):
    t, d = x.shape
    d_ff = w1.shape[1]
    row = lambda i, j: (i, 0)
    const = lambda i, j: (0, 0)
    out_dtypes = [F32, BF16] if with_bf16 else [F32]
    return pl.pallas_call(
        functools.partial(_ffn_kernel, alpha=alpha),
        grid=(t // tm, d_ff // tf),
        in_specs=[pl.BlockSpec((tm, d), row), pl.BlockSpec((tm, d), row),
                  pl.BlockSpec((d, tf), lambda i, j: (0, j)), pl.BlockSpec((tf, d), lambda i, j: (j, 0)),
                  pl.BlockSpec((1, d), const), pl.BlockSpec((1, d), const)],
        out_specs=[pl.BlockSpec((tm, d), row) for _ in out_dtypes],
        out_shape=[jax.ShapeDtypeStruct((t, d), dt) for dt in out_dtypes],
        compiler_params=_params("parallel", "arbitrary"),
        name="ffn",
    )(x, xb, w1, w2, ln_g.reshape(1, -1), ln_b.reshape(1, -1))


def kernel(x, mem, w_in, w_br_a, w_br_b, w_br_c, w_out, c_ln_g, c_ln_b, c_ws, c_bs,
           ln1_g, ln1_b, w_xq, w_xk, w_xv, w_xo, ln2_g, ln2_b, w_ff1, w_ff2, ln3_g, ln3_b):
    batch, seq, d = x.shape
    depth = w_in.shape[0]
    t = batch * seq
    alpha = (2 * depth) ** 0.25
    assert all(seq % window == 0 for window, _ in A_PATTERNS) and seq % ATT_BLOCK == 0
    assert w_in.shape[2] == 3 * ATT_WIDTH * 2 + 2 * C_WIDTH + 3 * d

    qa_blk, qb_blk = 0, 3 * ATT_WIDTH // LANES
    u_blk, v_blk = 6, 7
    gate_blk = (6 * ATT_WIDTH + 2 * C_WIDTH) // d

    xf = x.reshape(t, d)
    xb = xf
    wa, wb, wc, wo, wq, wxo = (w.astype(BF16) for w in (w_br_a, w_br_b, w_br_c, w_out, w_xq, w_xo))
    w_kv = jnp.concatenate([w for l in range(depth) for w in (w_xk[l], w_xv[l])], axis=1).astype(BF16)
    kv = _matmul(mem.reshape(-1, d).astype(BF16), w_kv, BF16, *TILES["mem_proj"], "mem_proj")
    kv = kv.reshape(batch, -1, depth * 2 * X_WIDTH)
    for l in range(depth):
        proj2d = _matmul_f32w(xb, w_in, l, BF16, *TILES["in_proj"], "in_proj")
        proj = proj2d.reshape(batch, seq, -1)
        oa, w1 = _attention(proj, qa_blk, "dilated", w_ff1, l, "dilated_attn")
        ob, w2 = _attention(proj, qb_blk, "moba", w_ff2, l, "moba_attn")
        xf, xb = _merge(oa, ob, proj2d, u_blk, v_blk, gate_blk, xf,
                        c_ln_g[l], c_ln_b[l], c_ws[l], c_bs[l],
                        l, wa, wb, wc, wo, ln1_g[l], ln1_b[l], alpha, TILES["merge"])
        xf, xb = _cross(xf, xb, kv, l, wq, wxo, ln2_g[l], ln2_b[l], alpha, batch, TILES["cross"])
        last = l == depth - 1
        outs = _ffn(xf, xb, w1, w2,
                    ln3_g[l], ln3_b[l], alpha, *TILES["ffn"], not last)
        xf = outs[0]
        xb = None if last else outs[1]
    return xf.reshape(batch, seq, d)
```
